```python
import math
import jax
import jax.numpy as jnp
from jax import lax
import numpy as np

D_MODEL = 1024
BATCH = 8
SEQ = 4096
DEPTH = 1
DEC_BATCH = 32
DEC_SEQ = 4
PAST_LEN = 16384
PAGE_SIZE = 128

A_GROUPS = ((128, 1), (512, 4), (2048, 16))
A_HEADS_PER_GROUP = 8
A_HEAD_DIM = 64
A_N_HEADS = len(A_GROUPS) * A_HEADS_PER_GROUP
A_QKV_WIDTH = A_N_HEADS * A_HEAD_DIM
A_OUT_WIDTH = A_HEADS_PER_GROUP * A_HEAD_DIM
A_QBLOCK = 128
N_BUCKETS = 32
BUCKET_MAX_DIST = 2048
B_HEADS = 8
B_HEAD_DIM = 128
B_WIDTH = B_HEADS * B_HEAD_DIM
CONV_W = 4
B_CHUNK = 64
ALPHA = (2 * DEPTH) ** 0.25
BETA_INIT = (8 * DEPTH) ** -0.25
LN_EPS = 1e-5
NORM_EPS = 1e-6
PROJ_SPLITS = (A_QKV_WIDTH, A_QKV_WIDTH, A_QKV_WIDTH, A_OUT_WIDTH, 3 * B_WIDTH, B_WIDTH, B_HEADS, B_HEADS, D_MODEL, D_MODEL)
PROJ_WIDTH = sum(PROJ_SPLITS)

kernel_name = 'hybrid_dilated_gdn_decoder_step'


def t5_causal_buckets(dist):
    max_exact = N_BUCKETS // 2
    dist = np.asarray(dist, dtype=np.int64)
    ratio = np.maximum(dist, max_exact) / max_exact
    large = max_exact + (np.log(ratio) / math.log(BUCKET_MAX_DIST / max_exact) * (N_BUCKETS - max_exact)).astype(np.int64)
    return np.where(dist < max_exact, dist, np.minimum(large, N_BUCKETS - 1)).astype(np.int32)


def group_biases(rel_bias):
    out = []
    for gi, (window, dil) in enumerate(A_GROUPS):
        buckets = t5_causal_buckets(dil * np.arange(window // dil + 1))
        cols = rel_bias[buckets][:, gi * A_HEADS_PER_GROUP:(gi + 1) * A_HEADS_PER_GROUP]
        out.append(cols.T)
    return out


def dilated_attend(q, k_slab, v_slab, pos0, dilation, bias_hj):
    n_q = q.shape[1]
    window = k_slab.shape[1] - n_q
    n_keys = window // dilation + 1
    idx = (window + np.arange(n_q)[:, None] - dilation * np.arange(n_keys)[None, :]).astype(np.int32)
    valid = (pos0 + idx) >= 0
    k_sel = k_slab[:, idx]
    v_sel = v_slab[:, idx]
    logits = jnp.einsum('bqhd,bqjhd->bqhj', q, k_sel, preferred_element_type=jnp.float32) * (A_HEAD_DIM ** -0.5)
    logits = logits + bias_hj.astype(jnp.float32)
    logits = jnp.where(valid[:, None, :], logits, -jnp.inf)
    m = jnp.max(logits, axis=-1, keepdims=True)
    p = jnp.exp(logits - m)
    s = jnp.sum(p, axis=-1, keepdims=True)
    out = jnp.einsum('bqhj,bqjhd->bqhd', p / s, v_sel.astype(jnp.float32))
    lse = (m + jnp.log(s))[..., 0]
    return out, lse


def dilated_prompt(q, k, v, window, dilation, bias_hj):
    bsz, seq, nh, hd = q.shape
    pad = ((0, 0), (window, 0), (0, 0), (0, 0))
    kp = jnp.pad(k, pad)
    vp = jnp.pad(v, pad)

    def block(t0):
        qb = lax.dynamic_slice_in_dim(q, t0, A_QBLOCK, axis=1)
        ks = lax.dynamic_slice_in_dim(kp, t0, window + A_QBLOCK, axis=1)
        vs = lax.dynamic_slice_in_dim(vp, t0, window + A_QBLOCK, axis=1)
        return dilated_attend(qb, ks, vs, t0 - window, dilation, bias_hj)

    starts = jnp.arange(seq // A_QBLOCK, dtype=jnp.int32) * A_QBLOCK
    out, lse = lax.map(block, starts)
    out = jnp.moveaxis(out, 0, 1).reshape(bsz, seq, nh, hd)
    lse = jnp.moveaxis(lse, 0, 1).reshape(bsz, seq, nh)
    return out, lse


def combine_groups(outs, lses):
    weights = jax.nn.softmax(jnp.stack(lses, 0), axis=0)
    return jnp.einsum('gbth,gbthd->bthd', weights, jnp.stack(outs, 0))


def short_conv(ext, w):
    n = ext.shape[1] - (CONV_W - 1)
    return sum(ext[:, i:i + n] * w[i] for i in range(CONV_W))


def l2norm(t):
    return t * lax.rsqrt(jnp.sum(t * t, axis=-1, keepdims=True) + NORM_EPS)


def gated_delta_chunk(state, q, k, v, g, beta):
    n = q.shape[2]
    causal = np.tril(np.ones((n, n), dtype=bool))
    strict = np.tril(np.ones((n, n), dtype=bool), -1)
    cum = jnp.cumsum(g, axis=-1)
    decay = jnp.exp(jnp.where(causal, cum[..., :, None] - cum[..., None, :], -jnp.inf))
    kk = jnp.einsum('bhid,bhjd->bhij', k, k)
    lower = jnp.where(strict, beta[..., :, None] * kk * decay, 0.0) + jnp.eye(n, dtype=q.dtype)
    rhs = jnp.concatenate([beta[..., None] * v, beta[..., None] * k * jnp.exp(cum)[..., None]], axis=-1)
    sol = lax.linalg.triangular_solve(lower, rhs, left_side=True, lower=True, unit_diagonal=True)
    u, w = sol[..., :B_HEAD_DIM], sol[..., B_HEAD_DIM:]
    v_new = u - jnp.einsum('bhid,bhde->bhie', w, state)
    qk = jnp.einsum('bhid,bhjd->bhij', q, k) * decay
    out = jnp.einsum('bhid,bhde->bhie', q * jnp.exp(cum)[..., None], state) + jnp.einsum('bhij,bhje->bhie', qk, v_new)
    tail = jnp.exp(cum[..., -1:] - cum)
    new_state = jnp.exp(cum[..., -1])[..., None, None] * state + jnp.einsum('bhid,bhie->bhde', k * tail[..., None], v_new)
    return new_state, out


def gated_delta(state, q, k, v, g, beta):
    bsz, nh, seq = g.shape
    size = B_CHUNK if seq % B_CHUNK == 0 else seq
    n = seq // size

    def chunks(t):
        return jnp.moveaxis(t.reshape((bsz, nh, n, size) + t.shape[3:]), 2, 0)

    state, out = lax.scan(lambda s, xs: gated_delta_chunk(s, *xs), state,
                          (chunks(q), chunks(k), chunks(v), chunks(g), chunks(beta)))
    out = jnp.moveaxis(out, 0, 2).reshape(bsz, nh, seq, B_HEAD_DIM)
    return state, out


def delta_branch(qkv_ext, a_b, b_b, state0, conv_w, a_log, dt_bias):
    qkv = jax.nn.silu(short_conv(qkv_ext, conv_w)).astype(jnp.float32)
    bsz, seq, _ = qkv.shape
    q, k, v = [t.reshape(bsz, seq, B_HEADS, B_HEAD_DIM).transpose(0, 2, 1, 3) for t in jnp.split(qkv, 3, axis=-1)]
    q = l2norm(q) * (B_HEAD_DIM ** -0.5)
    k = l2norm(k)
    beta = jax.nn.sigmoid(b_b.astype(jnp.float32)).transpose(0, 2, 1)
    g = (-jnp.exp(a_log.astype(jnp.float32)) * jax.nn.softplus(a_b.astype(jnp.float32) + dt_bias.astype(jnp.float32))).transpose(0, 2, 1)
    return gated_delta(state0.astype(jnp.float32), q, k, v, g, beta)


def split_columns(p):
    out, start = [], 0
    for width in PROJ_SPLITS:
        out.append(p[..., start:start + width])
        start += width
    return out


def split_heads_a(t):
    return t.reshape(t.shape[:2] + (len(A_GROUPS), A_HEADS_PER_GROUP, A_HEAD_DIM))


def layer_front(x, c, w_cond, b_cond, w_in):
    shift, scale, gate = jnp.split(jax.nn.silu(c) @ w_cond + b_cond, 3, axis=-1)
    h = x * (1 + scale[:, None]) + shift[:, None]
    return gate, split_columns(h @ w_in)


def layer_back(x, gate, o_a, za, o_b, zb, ga, gb, b_norm_w, w_branch_a, w_branch_b, w_out, ln_g, ln_b):
    dtype = x.dtype
    bsz, seq, _ = x.shape
    ya = o_a.reshape(bsz, seq, A_OUT_WIDTH).astype(dtype) * jax.nn.silu(za)
    ob = jnp.swapaxes(o_b, 1, 2)
    ob = ob * lax.rsqrt(jnp.mean(ob * ob, axis=-1, keepdims=True) + NORM_EPS) * b_norm_w.astype(jnp.float32)
    yb = ob.reshape(bsz, seq, B_WIDTH).astype(dtype) * jax.nn.silu(zb)
    merged = jax.nn.sigmoid(ga) * (ya @ w_branch_a) + jax.nn.sigmoid(gb) * (yb @ w_branch_b)
    r = (ALPHA * x + gate[:, None] * (merged @ w_out)).astype(jnp.float32)
    mu = jnp.mean(r, axis=-1, keepdims=True)
    var = jnp.mean(jnp.square(r - mu), axis=-1, keepdims=True)
    return ((r - mu) * lax.rsqrt(var + LN_EPS)).astype(dtype) * ln_g + ln_b


def prompt_layer(x, c, biases, w_cond, b_cond, w_in, conv_w, a_log, dt_bias, b_norm_w, w_branch_a, w_branch_b, w_out, ln_g, ln_b):
    gate, (qa, ka, va, za, qkv_b, zb, a_b, b_b, ga, gb) = layer_front(x, c, w_cond, b_cond, w_in)
    qa, ka, va = split_heads_a(qa), split_heads_a(ka), split_heads_a(va)
    bsz, seq, _ = x.shape
    outs, lses, kv_new = [], [], []
    for gi, (window, dil) in enumerate(A_GROUPS):
        o, l = dilated_prompt(qa[:, :, gi], ka[:, :, gi], va[:, :, gi], window, dil, biases[gi])
        outs.append(o)
        lses.append(l)
        kv_new.append(jnp.stack([ka[:, :, gi], va[:, :, gi]], axis=2)[:, seq - min(window, seq):])
    o_a = combine_groups(outs, lses)
    qkv_ext = jnp.concatenate([jnp.zeros((bsz, CONV_W - 1, qkv_b.shape[-1]), qkv_b.dtype), qkv_b], axis=1)
    state0 = jnp.zeros((bsz, B_HEADS, B_HEAD_DIM, B_HEAD_DIM), jnp.float32)
    state, o_b = delta_branch(qkv_ext, a_b, b_b, state0, conv_w, a_log, dt_bias)
    y = layer_back(x, gate, o_a, za, o_b, zb, ga, gb, b_norm_w, w_branch_a, w_branch_b, w_out, ln_g, ln_b)
    return y, (kv_new[0], kv_new[1], kv_new[2], qkv_ext[:, -(CONV_W - 1):], state.astype(x.dtype))


def sample_layer(x, c, kv_bufs, conv_state, delta_state, biases, w_cond, b_cond, w_in, conv_w, a_log, dt_bias, b_norm_w, w_branch_a, w_branch_b, w_out, ln_g, ln_b):
    gate, (qa, ka, va, za, qkv_b, zb, a_b, b_b, ga, gb) = layer_front(x, c, w_cond, b_cond, w_in)
    qa, ka, va = split_heads_a(qa), split_heads_a(ka), split_heads_a(va)
    outs, lses, kv_new = [], [], []
    for gi, (window, dil) in enumerate(A_GROUPS):
        buf = kv_bufs[gi].astype(x.dtype)
        n_buf = buf.shape[1]
        ext = jnp.concatenate([buf, jnp.stack([ka[:, :, gi], va[:, :, gi]], axis=2)], axis=1)
        ext_p = jnp.pad(ext, ((0, 0), (window - n_buf, 0), (0, 0), (0, 0), (0, 0)))
        o, l = dilated_attend(qa[:, :, gi], ext_p[:, :, 0], ext_p[:, :, 1], PAST_LEN - window, dil, biases[gi])
        outs.append(o)
        lses.append(l)
        kv_new.append(ext[:, ext.shape[1] - n_buf:].astype(kv_bufs[gi].dtype))
    o_a = combine_groups(outs, lses)
    qkv_ext = jnp.concatenate([conv_state.astype(qkv_b.dtype), qkv_b], axis=1)
    state, o_b = delta_branch(qkv_ext, a_b, b_b, delta_state, conv_w, a_log, dt_bias)
    y = layer_back(x, gate, o_a, za, o_b, zb, ga, gb, b_norm_w, w_branch_a, w_branch_b, w_out, ln_g, ln_b)
    return y, (kv_new[0], kv_new[1], kv_new[2], qkv_ext[:, -(CONV_W - 1):].astype(conv_state.dtype), state.astype(delta_state.dtype))


def setup_inputs(seed: int = 0) -> dict:
    key = jax.random.key(seed)
    ks = jax.random.split(key, 24)
    f32 = jnp.float32
    d = D_MODEL

    def nrm(k, shape, s=1.0):
        return jax.random.normal(k, shape, f32) * s

    kv_shape = lambda w: (DEPTH, DEC_BATCH, min(w, PAST_LEN), 2, A_HEADS_PER_GROUP, A_HEAD_DIM)
    dt = jnp.exp(jax.random.uniform(ks[15], (DEPTH, B_HEADS), f32, math.log(1e-3), math.log(1e-1)))
    return {
        'x_prompt': nrm(ks[0], (BATCH, SEQ, d)),
        'x_sample': nrm(ks[1], (DEC_BATCH, DEC_SEQ, d)),
        'c_prompt': nrm(ks[2], (BATCH, d)),
        'c_sample': nrm(ks[3], (DEC_BATCH, d)),
        'cache_kv_w128': nrm(ks[4], kv_shape(A_GROUPS[0][0])),
        'cache_kv_w512': nrm(ks[5], kv_shape(A_GROUPS[1][0])),
        'cache_kv_w2048': nrm(ks[6], kv_shape(A_GROUPS[2][0])),
        'state_conv': nrm(ks[7], (DEPTH, DEC_BATCH, CONV_W - 1, 3 * B_WIDTH)),
        'state_delta': nrm(ks[8], (DEPTH, DEC_BATCH, B_HEADS, B_HEAD_DIM, B_HEAD_DIM), B_HEAD_DIM ** -0.5),
        'w_cond': nrm(ks[9], (DEPTH, d, 3 * d), 0.5 * d ** -0.5),
        'b_cond': nrm(ks[10], (DEPTH, 3 * d), 0.01),
        'w_in': nrm(ks[11], (DEPTH, d, PROJ_WIDTH), d ** -0.5),
        'rel_bias': nrm(ks[12], (N_BUCKETS, A_N_HEADS), 0.5),
        'conv_w': nrm(ks[13], (DEPTH, CONV_W, 3 * B_WIDTH), CONV_W ** -0.5),
        'a_log': jnp.log(jax.random.uniform(ks[14], (DEPTH, B_HEADS), f32, 1.0, 16.0)),
        'dt_bias': dt + jnp.log(-jnp.expm1(-dt)),
        'b_norm_w': 1.0 + nrm(ks[16], (DEPTH, B_HEAD_DIM), 0.01),
        'w_branch_a': nrm(ks[17], (DEPTH, A_OUT_WIDTH, d), BETA_INIT * A_OUT_WIDTH ** -0.5),
        'w_branch_b': nrm(ks[18], (DEPTH, B_WIDTH, d), BETA_INIT * B_WIDTH ** -0.5),
        'w_out': nrm(ks[19], (DEPTH, d, d), BETA_INIT * d ** -0.5),
        'ln_g': 1.0 + nrm(ks[20], (DEPTH, d), 0.01),
        'ln_b': nrm(ks[21], (DEPTH, d), 0.01),
    }


def reference(x_prompt, x_sample, c_prompt, c_sample, cache_kv_w128, cache_kv_w512, cache_kv_w2048, state_conv, state_delta,
              w_cond, b_cond, w_in, rel_bias, conv_w, a_log, dt_bias, b_norm_w, w_branch_a, w_branch_b, w_out, ln_g, ln_b):
    biases = group_biases(rel_bias)
    xp, xs = x_prompt, x_sample
    p_states, s_states = [], []
    for layer in range(DEPTH):
        lw = (w_cond[layer], b_cond[layer], w_in[layer], conv_w[layer], a_log[layer], dt_bias[layer], b_norm_w[layer],
              w_branch_a[layer], w_branch_b[layer], w_out[layer], ln_g[layer], ln_b[layer])
        xp, sp = prompt_layer(xp, c_prompt, biases, *lw)
        xs, ss = sample_layer(xs, c_sample, (cache_kv_w128[layer], cache_kv_w512[layer], cache_kv_w2048[layer]),
                              state_conv[layer], state_delta[layer], biases, *lw)
        p_states.append(sp)
        s_states.append(ss)
    kv128_p, kv512_p, kv2048_p, conv_p, delta_p = [jnp.stack(t, 0) for t in zip(*p_states)]
    kv128_s, kv512_s, kv2048_s, conv_s, delta_s = [jnp.stack(t, 0) for t in zip(*s_states)]
    return (xp, xs, kv128_p, kv512_p, kv2048_p, conv_p, delta_p, kv128_s, kv512_s, kv2048_s, conv_s, delta_s)
```

```python
import functools
import math

import numpy as np
import jax
import jax.numpy as jnp
from jax import lax
from jax.experimental import pallas as pl
from jax.experimental.pallas import tpu as pltpu

F32 = jnp.float32
BF16 = jnp.bfloat16
HIGHEST = lax.Precision.HIGHEST

D_MODEL = 1024
A_GROUPS = ((128, 1), (512, 4), (2048, 16))
A_HEADS = 8
A_HEAD_DIM = 64
A_GROUP_WIDTH = A_HEADS * A_HEAD_DIM
A_KEYS = 128
N_BUCKETS = 32
BUCKET_MAX_DIST = 2048
B_HEADS = 8
B_HEAD_DIM = 128
B_WIDTH = B_HEADS * B_HEAD_DIM
CONV_W = 4
B_CHUNK = 64
DEPTH = 1
ALPHA = (2 * DEPTH) ** 0.25
LN_EPS = 1e-5
NORM_EPS = 1e-6
NEG = -1e30

LANES = 128
SUBLANES = 8

COL_QA = 0
COL_KA = 1536
COL_VA = 3072
COL_ZA = 4608
COL_QKVB = 5120
COL_ZB = 8192
COL_GA = 9216
COL_GB = 10240
COL_AB = 11264
PROJ_COLS = 11520
PROJ_TN = 1280

VMEM_LIMIT = 48 * 1024 * 1024


def _sigmoid(x):
    return 1.0 / (1.0 + jnp.exp(-x))


def _silu(x):
    return x * _sigmoid(x)


def _softplus(x):
    return jnp.maximum(x, 0.0) + jnp.log1p(jnp.exp(-jnp.abs(x)))


def _cond_kernel(c_ref, w_ref, b_ref, o_ref):
    s = _silu(c_ref[...]).astype(BF16)
    o_ref[...] = jnp.dot(s, w_ref[...].astype(BF16), preferred_element_type=F32) + b_ref[...]


def _cond(c, w, b):
    n, d = c.shape
    width = w.shape[1]
    tn = 512
    return pl.pallas_call(
        _cond_kernel,
        grid=(width // tn,),
        in_specs=[pl.BlockSpec((n, d), lambda j: (0, 0)),
                  pl.BlockSpec((d, tn), lambda j: (0, j)),
                  pl.BlockSpec((1, tn), lambda j: (0, j))],
        out_specs=pl.BlockSpec((n, tn), lambda j: (0, j)),
        out_shape=jax.ShapeDtypeStruct((n, width), F32),
        name="cond",
    )(c, w, b.reshape(1, width))


def _proj_kernel(x_ref, sc_ref, sh_ref, w_ref, o_ref, h_ref):
    @pl.when(pl.program_id(1) == 0)
    def _():
        h_ref[...] = (x_ref[...] * (1.0 + sc_ref[...]) + sh_ref[...]).astype(BF16)

    o_ref[...] = jnp.dot(h_ref[...], w_ref[...], preferred_element_type=F32)


def _proj(x, scale, shift, w, tm):
    m, d = x.shape
    n = w.shape[1]
    groups, r, _ = scale.shape
    blocks_per_group = (m // tm) // groups
    mod_spec = pl.BlockSpec((None, r, d), lambda i, j: (i // blocks_per_group, 0, 0))
    return pl.pallas_call(
        _proj_kernel,
        grid=(m // tm, n // PROJ_TN),
        in_specs=[pl.BlockSpec((tm, d), lambda i, j: (i, 0)),
                  mod_spec, mod_spec,
                  pl.BlockSpec((d, PROJ_TN), lambda i, j: (0, j))],
        out_specs=pl.BlockSpec((tm, PROJ_TN), lambda i, j: (i, j)),
        out_shape=jax.ShapeDtypeStruct((m, n), F32),
        scratch_shapes=[pltpu.VMEM((tm, d), BF16)],
        compiler_params=pltpu.CompilerParams(
            dimension_semantics=("arbitrary", "arbitrary"), vmem_limit_bytes=VMEM_LIMIT),
        name="proj",
    )(x, scale, shift, w)


ATT_TB = 2048


def _attn_prompt_kernel(q_ref, k_ref, kp_ref, v_ref, vp_ref, tb_ref, o_ref,
                        kc_ref, vc_ref, acc_ref, m_ref, l_ref):
    i = pl.program_id(1)
    gi = pl.program_id(3)
    tb = ATT_TB
    kc_ref[0:tb, :] = kp_ref[...]
    kc_ref[tb:2 * tb, :] = k_ref[...]
    vc_ref[0:tb, :] = vp_ref[...]
    vc_ref[tb:2 * tb, :] = v_ref[...]

    lane = lax.broadcasted_iota(jnp.int32, (A_KEYS, LANES), 1)
    lo = lane < A_HEAD_DIM
    colk = lax.broadcasted_iota(jnp.int32, (A_KEYS, 2 * A_KEYS), 1)

    def run_group(d, first, last):
        tq = A_KEYS * d
        shift = int(math.log2(d))

        def body(u, carry):
            sb = u >> shift
            r = u & (d - 1)
            start = sb * tq + r
            rows = pl.ds(start, A_KEYS, stride=d)
            q = q_ref[rows, :]
            krows = pl.ds(tb - tq + start, 2 * A_KEYS, stride=d)
            kk = kc_ref[krows, :].astype(BF16)
            vv = vc_ref[krows, :].astype(BF16)
            thr = jnp.where(jnp.logical_and(i == 0, sb == 0), A_KEYS, 0)
            res = []
            for hh in range(2):
                sel = lo if hh == 0 else jnp.logical_not(lo)
                qh = jnp.where(sel, q, 0.0).astype(BF16)
                s = lax.dot_general(qh, kk, (((1,), (1,)), ((), ())),
                                    preferred_element_type=F32) * (A_HEAD_DIM ** -0.5)
                s = s + tb_ref[hh]
                s = jnp.where(colk < thr, NEG, s)
                mx = jnp.max(s, axis=-1, keepdims=True)
                p = jnp.exp(s - mx)
                ls = jnp.sum(p, axis=-1, keepdims=True)
                o = jnp.dot(p.astype(BF16), vv, preferred_element_type=F32)
                res.append((o, mx, ls))
            o = jnp.where(lo, res[0][0], res[1][0])
            mx = jnp.where(lo, res[0][1], res[1][1])
            ls = jnp.where(lo, res[0][2], res[1][2])
            if first:
                acc_ref[rows, :] = o
                m_ref[rows, :] = mx
                l_ref[rows, :] = ls
            else:
                mo = m_ref[rows, :]
                mn = jnp.maximum(mo, mx)
                a = jnp.exp(mo - mn)
                b = jnp.exp(mx - mn)
                acc = a * acc_ref[rows, :] + b * o
                ll = a * l_ref[rows, :] + b * ls
                if last:
                    o_ref[rows, :] = acc / ll
                else:
                    acc_ref[rows, :] = acc
                    m_ref[rows, :] = mn
                    l_ref[rows, :] = ll
            return carry

        lax.fori_loop(0, tb // A_KEYS, body, 0)

    n_groups = len(A_GROUPS)
    for step in range(n_groups):
        g = n_groups - 1 - step

        @pl.when(gi == step)
        def _(g=g, step=step):
            run_group(A_GROUPS[g][1], step == 0, step == n_groups - 1)


def _attn_prompt(p3, tables):
    bsz, seq, _ = p3.shape
    nblk = seq // ATT_TB
    n_groups = len(A_GROUPS)
    hp_per_group = A_GROUP_WIDTH // LANES

    def col(base):
        return lambda b, i, hp, gi: (b, i, base // LANES + (n_groups - 1 - gi) * hp_per_group + hp)

    def col_prev(base):
        return lambda b, i, hp, gi: (b, jnp.maximum(i - 1, 0),
                                     base // LANES + (n_groups - 1 - gi) * hp_per_group + hp)

    blk = (None, ATT_TB, LANES)
    return pl.pallas_call(
        _attn_prompt_kernel,
        grid=(bsz, nblk, hp_per_group, n_groups),
        in_specs=[pl.BlockSpec(blk, col(COL_QA)),
                  pl.BlockSpec(blk, col(COL_KA)),
                  pl.BlockSpec(blk, col_prev(COL_KA)),
                  pl.BlockSpec(blk, col(COL_VA)),
                  pl.BlockSpec(blk, col_prev(COL_VA)),
                  pl.BlockSpec((None, 2, A_KEYS, 2 * A_KEYS),
                               lambda b, i, hp, gi: (n_groups - 1 - gi, hp, 0, 0))],
        out_specs=pl.BlockSpec(blk, lambda b, i, hp, gi: (b, i, hp)),
        out_shape=jax.ShapeDtypeStruct((bsz, seq, A_GROUP_WIDTH), F32),
        scratch_shapes=[pltpu.VMEM((2 * ATT_TB, LANES), F32),
                        pltpu.VMEM((2 * ATT_TB, LANES), F32),
                        pltpu.VMEM((ATT_TB, LANES), F32),
                        pltpu.VMEM((ATT_TB, LANES), F32),
                        pltpu.VMEM((ATT_TB, LANES), F32)],
        compiler_params=pltpu.CompilerParams(
            dimension_semantics=("arbitrary",) * 4, vmem_limit_bytes=VMEM_LIMIT),
        name="attn_prompt",
    )(p3, p3, p3, p3, p3, tables)


ROLL_ROWS = 16


def _attn_sample_kernel(qkv_ref, c0_ref, c1_ref, c2_ref, tc0_ref, tc1_ref, tc2_ref, tn_ref,
                        o_ref, n0_ref, n1_ref, n2_ref, nt_ref, *, n_new):
    caches = (c0_ref, c1_ref, c2_ref)
    tables = (tc0_ref, tc1_ref, tc2_ref)
    outs = (n0_ref, n1_ref, n2_ref)
    gw = A_GROUP_WIDTH
    scale = A_HEAD_DIM ** -0.5

    head_of_lane = lax.broadcasted_iota(jnp.int32, (A_HEADS, gw), 1) >> int(math.log2(A_HEAD_DIM))
    head_of_row = lax.broadcasted_iota(jnp.int32, (A_HEADS, gw), 0)
    hmask = head_of_lane == head_of_row
    lane = lax.broadcasted_iota(jnp.int32, (ROLL_ROWS, LANES), 1)
    is_new = lane >= LANES - n_new

    parts = []
    for g, (window, d) in enumerate(A_GROUPS):
        c_ref, n_ref = caches[g], outs[g]
        q_new = qkv_ref[:, COL_QA + g * gw:COL_QA + (g + 1) * gw]
        k_new = qkv_ref[:, COL_KA + g * gw:COL_KA + (g + 1) * gw]
        v_new = qkv_ref[:, COL_VA + g * gw:COL_VA + (g + 1) * gw]

        qbd = jnp.concatenate([jnp.where(hmask, q_new[i:i + 1, :], 0.0) for i in range(n_new)], axis=0)
        qb = qbd.astype(BF16)
        s_c = jnp.dot(qb, c_ref[0:gw, :].astype(BF16), preferred_element_type=F32) * scale + tables[g][...]
        qf = qb.astype(F32)
        kf = k_new.astype(BF16).astype(F32)
        vf = v_new.astype(BF16).astype(F32)
        s_n = [jnp.sum(qf * kf[j:j + 1, :], axis=-1, keepdims=True) * scale + tn_ref[g, :, j:j + 1]
               for j in range(n_new)]
        mx = functools.reduce(jnp.maximum, s_n, jnp.max(s_c, axis=-1, keepdims=True))
        p_c = jnp.exp(s_c - mx)
        p_n = [jnp.exp(s - mx) for s in s_n]
        ls = jnp.sum(p_c, axis=-1, keepdims=True) + sum(p_n)
        o = lax.dot_general(p_c.astype(BF16), c_ref[gw:2 * gw, :].astype(BF16), (((1,), (1,)), ((), ())),
                            preferred_element_type=F32)
        for j in range(n_new):
            o = o + p_n[j].astype(BF16).astype(F32) * vf[j:j + 1, :]
        parts.append((o, mx, ls))

        for cb in range(2 * gw // LANES):
            src = k_new if cb < gw // LANES else v_new
            c0 = (cb % (gw // LANES)) * LANES
            blk = jnp.concatenate([src[:, c0:c0 + LANES], jnp.zeros((LANES - src.shape[0], LANES), F32)], axis=0)
            nt_ref[cb * LANES:(cb + 1) * LANES, :] = pltpu.roll(blk.T, LANES - n_new, axis=1)

        def shift_rows(step, carry, c_ref=c_ref, n_ref=n_ref, window=window):
            r0 = pl.multiple_of(step * ROLL_ROWS, ROLL_ROWS)
            rows = pl.ds(r0, ROLL_ROWS)
            y = pltpu.roll(c_ref[rows, :], window - n_new, axis=1)
            if window > LANES:
                n_ref[rows, 0:window - LANES] = y[:, 0:window - LANES]
            n_ref[rows, window - LANES:window] = jnp.where(is_new, nt_ref[rows, :], y[:, window - LANES:window])
            return carry

        lax.fori_loop(0, 2 * gw // ROLL_ROWS, shift_rows, 0)

    mt = functools.reduce(jnp.maximum, [p[1] for p in parts])
    num = sum(jnp.exp(p[1] - mt) * p[0] for p in parts)
    den = sum(jnp.exp(p[1] - mt) * p[2] for p in parts)
    comb = num / den
    for i in range(n_new):
        sel = jnp.where(hmask, comb[i * A_HEADS:(i + 1) * A_HEADS, :], 0.0)
        o_ref[i:i + 1, :] = jnp.sum(sel, axis=0, keepdims=True)


def _attn_sample(qkv, n_new, c0, c1, c2, tabs_c, tab_n):
    dbsz, rows, qkv_w = qkv.shape
    nq = n_new * A_HEADS

    def cspec(w):
        return pl.BlockSpec((None, 2 * A_GROUP_WIDTH, w), lambda b: (b, 0, 0))

    ws = [g[0] for g in A_GROUPS]
    return pl.pallas_call(
        functools.partial(_attn_sample_kernel, n_new=n_new),
        grid=(dbsz,),
        in_specs=[pl.BlockSpec((None, rows, qkv_w), lambda b: (b, 0, 0)),
                  cspec(ws[0]), cspec(ws[1]), cspec(ws[2])]
        + [pl.BlockSpec((nq, w), lambda b: (0, 0)) for w in ws]
        + [pl.BlockSpec((len(A_GROUPS), nq, LANES), lambda b: (0, 0, 0))],
        out_specs=[pl.BlockSpec((None, n_new, A_GROUP_WIDTH), lambda b: (b, 0, 0)),
                   cspec(ws[0]), cspec(ws[1]), cspec(ws[2])],
        out_shape=[jax.ShapeDtypeStruct((dbsz, n_new, A_GROUP_WIDTH), F32)]
        + [jax.ShapeDtypeStruct(c.shape, c.dtype) for c in (c0, c1, c2)],
        scratch_shapes=[pltpu.VMEM((2 * A_GROUP_WIDTH, LANES), F32)],
        compiler_params=pltpu.CompilerParams(
            dimension_semantics=("arbitrary",), vmem_limit_bytes=VMEM_LIMIT),
        name="attn_sample",
    )(qkv, c0, c1, c2, *tabs_c, tab_n)


def _delta_kernel(xq_ref, xk_ref, xv_ref, ab_ref, hist_ref, s0_ref, cw_ref, gp_ref, o_ref, st_ref, xs_ref,
                  *, n_valid):
    c = pl.program_id(1)
    cs = xq_ref.shape[0]
    hd = B_HEAD_DIM

    @pl.when(c == 0)
    def _():
        xs_ref[0:SUBLANES, :] = hist_ref[...]
        st_ref[...] = s0_ref[...]

    for part, ref in enumerate((xq_ref, xk_ref, xv_ref)):
        xs_ref[SUBLANES:SUBLANES + cs, part * B_WIDTH:(part + 1) * B_WIDTH] = ref[...]

    def conv_silu(col0):
        cols = slice(col0, col0 + hd)
        y = xs_ref[pl.ds(SUBLANES - 3, cs), cols] * cw_ref[0:1, cols]
        for t in range(1, CONV_W):
            y = y + xs_ref[pl.ds(SUBLANES - 3 + t, cs), cols] * cw_ref[t:t + 1, cols]
        return _silu(y)

    def l2norm(t):
        return t * lax.rsqrt(jnp.sum(t * t, axis=-1, keepdims=True) + NORM_EPS)

    ab = ab_ref[...]
    g_all = -jnp.exp(gp_ref[0:1, :]) * _softplus(ab + gp_ref[1:2, :])
    beta_all = _sigmoid(ab)
    if n_valid < cs:
        live = lax.broadcasted_iota(jnp.int32, (cs, LANES), 0) < n_valid
        g_all = jnp.where(live, g_all, 0.0)
        beta_all = jnp.where(live, beta_all, 0.0)
    ri = lax.broadcasted_iota(jnp.int32, (cs, cs), 0)
    ci = lax.broadcasted_iota(jnp.int32, (cs, cs), 1)
    causal = ri >= ci
    strict = ri > ci
    eye = (ri == ci).astype(F32)
    cum = jnp.dot(causal.astype(F32), g_all, precision=HIGHEST, preferred_element_type=F32)
    cum_sq = jnp.concatenate([cum, jnp.zeros((LANES - cs, LANES), F32)], axis=0) if cs < LANES else cum
    cum_t = cum_sq.T
    ecum = jnp.exp(cum)
    cum_last = cum[cs - 1:cs, :]
    tail = jnp.exp(cum_last - cum)
    elast = jnp.exp(cum_last)

    n_sq = int(math.log2(cs)) - 1
    for h in range(B_HEADS):
        qh = l2norm(conv_silu(h * hd)) * (hd ** -0.5)
        kh = l2norm(conv_silu(B_WIDTH + h * hd))
        vh = conv_silu(2 * B_WIDTH + h * hd)
        beta = beta_all[:, SUBLANES + h:SUBLANES + h + 1]
        ec = ecum[:, h:h + 1]
        decay = jnp.exp(jnp.where(causal, cum[:, h:h + 1] - cum_t[h:h + 1, 0:cs], NEG))
        kb = kh.astype(BF16)
        kk = lax.dot_general(kb, kb, (((1,), (1,)), ((), ())), preferred_element_type=F32)
        nl = -jnp.where(strict, beta * kk * decay, 0.0)
        tinv = eye + nl
        pw = nl
        for _ in range(n_sq):
            pw = jnp.dot(pw, pw, precision=HIGHEST, preferred_element_type=F32)
            tinv = tinv + jnp.dot(tinv, pw, precision=HIGHEST, preferred_element_type=F32)
        rhs = jnp.concatenate([beta * vh, beta * kh * ec], axis=1)
        sol = jnp.dot(tinv, rhs, precision=HIGHEST, preferred_element_type=F32)
        u = sol[:, 0:hd]
        w = sol[:, hd:2 * hd]
        state = st_ref[h]
        sb = state.astype(BF16)
        v_new = u - jnp.dot(w.astype(BF16), sb, preferred_element_type=F32)
        vb = v_new.astype(BF16)
        qk = lax.dot_general(qh.astype(BF16), kb, (((1,), (1,)), ((), ())), preferred_element_type=F32) * decay
        out = jnp.dot((qh * ec).astype(BF16), sb, preferred_element_type=F32)
        out = out + jnp.dot(qk.astype(BF16), vb, preferred_element_type=F32)
        o_ref[:, h * hd:(h + 1) * hd] = out
        kt = (kh * tail[:, h:h + 1]).astype(BF16)
        st_ref[h] = elast[:, h:h + 1] * state + lax.dot_general(
            kt, vb, (((0,), (0,)), ((), ())), preferred_element_type=F32)

    xs_ref[0:SUBLANES, :] = xs_ref[cs:cs + SUBLANES, :]


def _delta(p3, col_qkv, col_ab, hist, state0, conv_w, gate_par, cs, n_valid):
    nb, t, _ = p3.shape
    n_chunks = t // cs
    qkv_w = 3 * B_WIDTH

    def xspec(part):
        return pl.BlockSpec((None, cs, B_WIDTH), lambda b, c: (b, c, col_qkv // B_WIDTH + part))

    return pl.pallas_call(
        functools.partial(_delta_kernel, n_valid=n_valid),
        grid=(nb, n_chunks),
        in_specs=[xspec(0), xspec(1), xspec(2),
                  pl.BlockSpec((None, cs, LANES), lambda b, c: (b, c, col_ab // LANES)),
                  pl.BlockSpec((None, SUBLANES, qkv_w), lambda b, c: (b, 0, 0)),
                  pl.BlockSpec((None, B_HEADS, B_HEAD_DIM, B_HEAD_DIM), lambda b, c: (b, 0, 0, 0)),
                  pl.BlockSpec((CONV_W, qkv_w), lambda b, c: (0, 0)),
                  pl.BlockSpec((SUBLANES, LANES), lambda b, c: (0, 0))],
        out_specs=[pl.BlockSpec((None, cs, B_WIDTH), lambda b, c: (b, c, 0)),
                   pl.BlockSpec((None, B_HEADS, B_HEAD_DIM, B_HEAD_DIM), lambda b, c: (b, 0, 0, 0))],
        out_shape=[jax.ShapeDtypeStruct((nb, t, B_WIDTH), F32),
                   jax.ShapeDtypeStruct((nb, B_HEADS, B_HEAD_DIM, B_HEAD_DIM), F32)],
        scratch_shapes=[pltpu.VMEM((SUBLANES + cs, qkv_w), F32)],
        compiler_params=pltpu.CompilerParams(
            dimension_semantics=("arbitrary", "arbitrary"), vmem_limit_bytes=VMEM_LIMIT),
        name="delta",
    )(p3, p3, p3, p3, hist, state0, conv_w, gate_par)


def _back_kernel(x_ref, gate_ref, oa_ref, za_ref, ob_ref, zb_ref, ga_ref, gb_ref,
                 wa_ref, wb_ref, wo_ref, bn_ref, lg_ref, lb_ref, y_ref):
    ya = (oa_ref[...] * _silu(za_ref[...])).astype(BF16)
    parts = []
    for h in range(B_HEADS):
        cols = slice(h * B_HEAD_DIM, (h + 1) * B_HEAD_DIM)
        t = ob_ref[:, cols]
        t = t * lax.rsqrt(jnp.mean(t * t, axis=-1, keepdims=True) + NORM_EPS) * bn_ref[...]
        parts.append((t * _silu(zb_ref[:, cols])).astype(BF16))
    yb = jnp.concatenate(parts, axis=1)
    ma = jnp.dot(ya, wa_ref[...], preferred_element_type=F32)
    mb = jnp.dot(yb, wb_ref[...], preferred_element_type=F32)
    merged = _sigmoid(ga_ref[...]) * ma + _sigmoid(gb_ref[...]) * mb
    r = ALPHA * x_ref[...] + gate_ref[...] * jnp.dot(merged.astype(BF16), wo_ref[...], preferred_element_type=F32)
    mu = jnp.mean(r, axis=-1, keepdims=True)
    rc = r - mu
    var = jnp.mean(rc * rc, axis=-1, keepdims=True)
    y_ref[...] = rc * lax.rsqrt(var + LN_EPS) * lg_ref[...] + lb_ref[...]


def _back(x, gate, oa, p, ob, wa, wb, wo, bn, lg, lb, tm):
    m, d = x.shape
    groups, r, _ = gate.shape
    blocks_per_group = (m // tm) // groups

    def const(shape):
        return pl.BlockSpec(shape, lambda i: (0,) * len(shape))

    return pl.pallas_call(
        _back_kernel,
        grid=(m // tm,),
        in_specs=[pl.BlockSpec((tm, d), lambda i: (i, 0)),
                  pl.BlockSpec((None, r, d), lambda i: (i // blocks_per_group, 0, 0)),
                  pl.BlockSpec((tm, A_GROUP_WIDTH), lambda i: (i, 0)),
                  pl.BlockSpec((tm, A_GROUP_WIDTH), lambda i: (i, COL_ZA // A_GROUP_WIDTH)),
                  pl.BlockSpec((tm, B_WIDTH), lambda i: (i, 0)),
                  pl.BlockSpec((tm, B_WIDTH), lambda i: (i, COL_ZB // B_WIDTH)),
                  pl.BlockSpec((tm, d), lambda i: (i, COL_GA // d)),
                  pl.BlockSpec((tm, d), lambda i: (i, COL_GB // d)),
                  const(wa.shape), const(wb.shape), const(wo.shape),
                  const((1, B_HEAD_DIM)), const((1, d)), const((1, d))],
        out_specs=pl.BlockSpec((tm, d), lambda i: (i, 0)),
        out_shape=jax.ShapeDtypeStruct((m, d), F32),
        compiler_params=pltpu.CompilerParams(
            dimension_semantics=("arbitrary",), vmem_limit_bytes=VMEM_LIMIT),
        name="back",
    )(x, gate, oa, p, ob, p, p, p, wa, wb, wo, bn, lg, lb)


def _t5_causal_buckets(dist):
    max_exact = N_BUCKETS // 2
    dist = np.asarray(dist, dtype=np.int64)
    ratio = np.maximum(dist, max_exact) / max_exact
    large = max_exact + (np.log(ratio) / math.log(BUCKET_MAX_DIST / max_exact) * (N_BUCKETS - max_exact)).astype(np.int64)
    return np.where(dist < max_exact, dist, np.minimum(large, N_BUCKETS - 1)).astype(np.int32)


def _group_bias(rel_bias, gi):
    window, dil = A_GROUPS[gi]
    buckets = _t5_causal_buckets(dil * np.arange(window // dil + 1))
    return rel_bias[buckets][:, gi * A_HEADS:(gi + 1) * A_HEADS].T


def _prompt_tables(rel_bias):
    qq = np.arange(A_KEYS)[:, None]
    kk = np.arange(2 * A_KEYS)[None, :]
    dist = qq - kk + A_KEYS
    valid = (dist >= 0) & (dist <= A_KEYS)
    dclip = np.clip(dist, 0, A_KEYS)
    tabs = []
    for gi in range(len(A_GROUPS)):
        tab = _group_bias(rel_bias, gi)[:, dclip]
        tabs.append(jnp.where(valid[None], tab, NEG))
    return jnp.stack(tabs).astype(F32)


def _sample_tables(rel_bias, n_new):
    tabs_c, tabs_n = [], []
    i_of_row = np.repeat(np.arange(n_new), A_HEADS)[:, None]
    h_of_row = np.tile(np.arange(A_HEADS), n_new)[:, None]
    for gi, (window, dil) in enumerate(A_GROUPS):
        b = _group_bias(rel_bias, gi)
        off = np.arange(window)[None, :] - i_of_row
        valid = (off >= 0) & (off % dil == 0)
        back = np.clip(A_KEYS - off // dil, 0, A_KEYS)
        tabs_c.append(jnp.where(valid, b[h_of_row, back], NEG).astype(F32))
        offn = i_of_row - np.arange(LANES)[None, :]
        validn = (offn >= 0) & (offn % dil == 0)
        backn = np.clip(offn // dil, 0, A_KEYS)
        tabs_n.append(jnp.where(validn, b[h_of_row, backn], NEG).astype(F32))
    return tabs_c, jnp.stack(tabs_n)


def _regroup_w_in(w_in):
    main = w_in[:, 0:9216]
    ab = w_in[:, 9216:9232]
    gates = w_in[:, 9232:11280]
    pad = jnp.zeros((w_in.shape[0], PROJ_COLS - 11280), w_in.dtype)
    return jnp.concatenate([main, gates, ab, pad], axis=1).astype(BF16)


def _kv_tail(p3, gi, rows):
    gw = A_GROUP_WIDTH
    k = p3[:, -rows:, COL_KA + gi * gw:COL_KA + (gi + 1) * gw]
    v = p3[:, -rows:, COL_VA + gi * gw:COL_VA + (gi + 1) * gw]
    bsz = p3.shape[0]
    return jnp.concatenate([k, v], axis=-1).reshape(bsz, rows, 2, A_HEADS, A_HEAD_DIM)


def kernel(x_prompt, x_sample, c_prompt, c_sample, cache_kv_w128, cache_kv_w512, cache_kv_w2048, state_conv, state_delta, w_cond, b_cond, w_in, rel_bias, conv_w, a_log, dt_bias, b_norm_w, w_branch_a, w_branch_b, w_out, ln_g, ln_b):
    bsz, seq, d = x_prompt.shape
    dbsz, dseq, _ = x_sample.shape
    layer = 0

    c_all = jnp.concatenate([c_prompt, c_sample], axis=0)
    c_all = jnp.pad(c_all, ((0, -(bsz + dbsz) % 16), (0, 0)))
    cond = _cond(c_all, w_cond[layer], b_cond[layer])
    shift, scale, gate = cond[:, 0:d], cond[:, d:2 * d], cond[:, 2 * d:3 * d]

    w_in_r = _regroup_w_in(w_in[layer])
    wa = w_branch_a[layer].astype(BF16)
    wb = w_branch_b[layer].astype(BF16)
    wo = w_out[layer].astype(BF16)
    bn = b_norm_w[layer].reshape(1, B_HEAD_DIM)
    lg = ln_g[layer].reshape(1, d)
    lb = ln_b[layer].reshape(1, d)
    gate_par = jnp.zeros((SUBLANES, LANES), F32)
    gate_par = gate_par.at[0, 0:B_HEADS].set(a_log[layer]).at[1, 0:B_HEADS].set(dt_bias[layer])
    cw = conv_w[layer]

    xp = x_prompt.reshape(bsz * seq, d)
    pp = _proj(xp, scale[0:bsz, None, :], shift[0:bsz, None, :], w_in_r, tm=1024)
    pp3 = pp.reshape(bsz, seq, PROJ_COLS)
    oa_p = _attn_prompt(pp3, _prompt_tables(rel_bias))
    hist0 = jnp.zeros((bsz, SUBLANES, 3 * B_WIDTH), F32)
    st0 = jnp.zeros((bsz, B_HEADS, B_HEAD_DIM, B_HEAD_DIM), F32)
    ob_p, st_p = _delta(pp3, COL_QKVB, COL_AB, hist0, st0, cw, gate_par, B_CHUNK, B_CHUNK)
    y_p = _back(xp, gate[0:bsz, None, :], oa_p.reshape(bsz * seq, A_GROUP_WIDTH), pp,
                ob_p.reshape(bsz * seq, B_WIDTH), wa, wb, wo, bn, lg, lb, tm=256)
    y_prompt = y_p.reshape(bsz, seq, d)
    kv_p = [_kv_tail(pp3, gi, min(w, seq))[None] for gi, (w, _) in enumerate(A_GROUPS)]
    conv_p = pp3[:, seq - (CONV_W - 1):, COL_QKVB:COL_QKVB + 3 * B_WIDTH][None]
    delta_p = st_p[None]

    ms = dbsz * dseq
    xs = x_sample.reshape(ms, d)
    rep = lambda t: jnp.repeat(t[bsz:bsz + dbsz], dseq, axis=0)[None]
    ps = _proj(xs, rep(scale), rep(shift), w_in_r, tm=ms)
    ps3 = ps.reshape(dbsz, dseq, PROJ_COLS)
    cache_in = (cache_kv_w128, cache_kv_w512, cache_kv_w2048)
    caches = [jnp.transpose(c[layer], (0, 2, 3, 4, 1)).reshape(dbsz, 2 * A_GROUP_WIDTH, c.shape[2]) for c in cache_in]
    qkv_s = jnp.pad(ps3[:, :, 0:COL_ZA], ((0, 0), (0, 16 - dseq), (0, 0)))
    tabs_c, tab_n = _sample_tables(rel_bias, dseq)
    oa_s, n0, n1, n2 = _attn_sample(qkv_s, dseq, caches[0], caches[1], caches[2], tabs_c, tab_n)
    kv_s = [jnp.transpose(n.reshape(dbsz, 2, A_HEADS, A_HEAD_DIM, n.shape[2]), (0, 4, 1, 2, 3))[None]
            for n in (n0, n1, n2)]
    cs_s = B_CHUNK
    pd = jnp.concatenate([ps3[:, :, COL_QKVB:COL_QKVB + 3 * B_WIDTH], ps3[:, :, COL_AB:COL_AB + LANES]], axis=-1)
    pd = jnp.pad(pd, ((0, 0), (0, cs_s - dseq), (0, 0)))
    hist_s = jnp.pad(state_conv[layer], ((0, 0), (SUBLANES - (CONV_W - 1), 0), (0, 0)))
    ob_s, st_s = _delta(pd, 0, 3 * B_WIDTH, hist_s, state_delta[layer], cw, gate_par, cs_s, dseq)
    y_s = _back(xs, rep(gate), oa_s.reshape(ms, A_GROUP_WIDTH), ps,
                ob_s[:, 0:dseq].reshape(ms, B_WIDTH), wa, wb, wo, bn, lg, lb, tm=ms)
    y_sample = y_s.reshape(dbsz, dseq, d)
    conv_s = ps3[:, dseq - (CONV_W - 1):, COL_QKVB:COL_QKVB + 3 * B_WIDTH][None]
    delta_s = st_s[None]

    return (y_prompt, y_sample, kv_p[0], kv_p[1], kv_p[2], conv_p, delta_p,
            kv_s[0], kv_s[1], kv_s[2], conv_s, delta_s)
```

```python
import functools
import math

import numpy as np
import jax
import jax.numpy as jnp
from jax import lax
from jax.experimental import pallas as pl
from jax.experimental.pallas import tpu as pltpu

F32 = jnp.float32
BF16 = jnp.bfloat16
HIGHEST = lax.Precision.HIGHEST

D_MODEL = 1024
A_GROUPS = ((128, 1), (512, 4), (2048, 16))
A_HEADS = 8
A_HEAD_DIM = 64
A_GROUP_WIDTH = A_HEADS * A_HEAD_DIM
A_KEYS = 128
N_BUCKETS = 32
BUCKET_MAX_DIST = 2048
B_HEADS = 8
B_HEAD_DIM = 128
B_WIDTH = B_HEADS * B_HEAD_DIM
CONV_W = 4
B_CHUNK = 64
DEPTH = 1
ALPHA = (2 * DEPTH) ** 0.25
LN_EPS = 1e-5
NORM_EPS = 1e-6
NEG = -1e30

LANES = 128
SUBLANES = 8

COL_QA = 0
COL_KA = 1536
COL_VA = 3072
COL_ZA = 4608
COL_QKVB = 5120
COL_ZB = 8192
COL_GA = 9216
COL_GB = 10240
COL_AB = 11264
PROJ_COLS = 11520
PROJ_TN = 1280

VMEM_LIMIT = 48 * 1024 * 1024


def _sigmoid(x):
    return 1.0 / (1.0 + jnp.exp(-x))


def _silu(x):
    return x * _sigmoid(x)


def _softplus(x):
    return jnp.maximum(x, 0.0) + jnp.log1p(jnp.exp(-jnp.abs(x)))


def _dot_split(a, b):
    a_hi = a.astype(BF16)
    a_lo = (a - a_hi.astype(F32)).astype(BF16)
    b_hi = b.astype(BF16)
    b_lo = (b - b_hi.astype(F32)).astype(BF16)
    lhs = jnp.concatenate([a_hi, a_lo, a_hi], axis=1)
    rhs = jnp.concatenate([b_hi, b_hi, b_lo], axis=0)
    return jnp.dot(lhs, rhs, preferred_element_type=F32)


def _cond_kernel(c_ref, w_ref, b_ref, o_ref):
    s = _silu(c_ref[...]).astype(BF16)
    o_ref[...] = jnp.dot(s, w_ref[...].astype(BF16), preferred_element_type=F32) + b_ref[...]


def _cond(c, w, b):
    n, d = c.shape
    width = w.shape[1]
    tn = 512
    return pl.pallas_call(
        _cond_kernel,
        grid=(width // tn,),
        in_specs=[pl.BlockSpec((n, d), lambda j: (0, 0)),
                  pl.BlockSpec((d, tn), lambda j: (0, j)),
                  pl.BlockSpec((1, tn), lambda j: (0, j))],
        out_specs=pl.BlockSpec((n, tn), lambda j: (0, j)),
        out_shape=jax.ShapeDtypeStruct((n, width), F32),
        name="cond",
    )(c, w, b.reshape(1, width))


def _proj_kernel(x_ref, sc_ref, sh_ref, w_ref, o_ref, h_ref):
    @pl.when(pl.program_id(1) == 0)
    def _():
        h_ref[...] = (x_ref[...] * (1.0 + sc_ref[...]) + sh_ref[...]).astype(BF16)

    o_ref[...] = jnp.dot(h_ref[...], w_ref[...], preferred_element_type=F32)


def _proj(x, scale, shift, w, tm):
    m, d = x.shape
    n = w.shape[1]
    groups, r, _ = scale.shape
    blocks_per_group = (m // tm) // groups
    mod_spec = pl.BlockSpec((None, r, d), lambda i, j: (i // blocks_per_group, 0, 0))
    return pl.pallas_call(
        _proj_kernel,
        grid=(m // tm, n // PROJ_TN),
        in_specs=[pl.BlockSpec((tm, d), lambda i, j: (i, 0)),
                  mod_spec, mod_spec,
                  pl.BlockSpec((d, PROJ_TN), lambda i, j: (0, j))],
        out_specs=pl.BlockSpec((tm, PROJ_TN), lambda i, j: (i, j)),
        out_shape=jax.ShapeDtypeStruct((m, n), F32),
        scratch_shapes=[pltpu.VMEM((tm, d), BF16)],
        compiler_params=pltpu.CompilerParams(
            dimension_semantics=("arbitrary", "arbitrary"), vmem_limit_bytes=VMEM_LIMIT),
        name="proj",
    )(x, scale, shift, w)


ATT_TB = 2048


def _attn_prompt_kernel(q_ref, k_ref, kp_ref, v_ref, vp_ref, tb_ref, o_ref,
                        kc_ref, vc_ref, acc_ref, m_ref, l_ref):
    i = pl.program_id(1)
    gi = pl.program_id(3)
    tb = ATT_TB
    kc_ref[0:tb, :] = kp_ref[...]
    kc_ref[tb:2 * tb, :] = k_ref[...]
    vc_ref[0:tb, :] = vp_ref[...]
    vc_ref[tb:2 * tb, :] = v_ref[...]

    lane = lax.broadcasted_iota(jnp.int32, (A_KEYS, LANES), 1)
    lo = lane < A_HEAD_DIM

    def run_group(d, first, last):
        tq = A_KEYS * d
        shift = int(math.log2(d))

        def body(u, carry):
            sb = u >> shift
            r = u & (d - 1)
            start = sb * tq + r
            rows = pl.ds(start, A_KEYS, stride=d)
            q = q_ref[rows, :]
            krows = pl.ds(tb - tq + start, 2 * A_KEYS, stride=d)
            kk = kc_ref[krows, :].astype(BF16)
            vv = vc_ref[krows, :].astype(BF16)
            variant = jnp.logical_and(i == 0, sb == 0).astype(jnp.int32)
            res = []
            for hh in range(2):
                sel = lo if hh == 0 else jnp.logical_not(lo)
                qh = jnp.where(sel, q * (A_HEAD_DIM ** -0.5), 0.0).astype(BF16)
                s = lax.dot_general(qh, kk, (((1,), (1,)), ((), ())), preferred_element_type=F32)
                s = s + tb_ref[variant, hh]
                mx = jnp.max(s, axis=-1, keepdims=True)
                p = jnp.exp(s - mx)
                ls = jnp.sum(p, axis=-1, keepdims=True)
                o = jnp.dot(p.astype(BF16), vv, preferred_element_type=F32)
                res.append((o, mx, ls))
            o = jnp.where(lo, res[0][0], res[1][0])
            mx = jnp.where(lo, res[0][1], res[1][1])
            ls = jnp.where(lo, res[0][2], res[1][2])
            if first:
                acc_ref[rows, :] = o
                m_ref[rows, :] = mx
                l_ref[rows, :] = ls
            else:
                mo = m_ref[rows, :]
                mn = jnp.maximum(mo, mx)
                a = jnp.exp(mo - mn)
                b = jnp.exp(mx - mn)
                acc = a * acc_ref[rows, :] + b * o
                ll = a * l_ref[rows, :] + b * ls
                if last:
                    o_ref[rows, :] = acc / ll
                else:
                    acc_ref[rows, :] = acc
                    m_ref[rows, :] = mn
                    l_ref[rows, :] = ll
            return carry

        lax.fori_loop(0, tb // A_KEYS, body, 0, unroll=4)

    n_groups = len(A_GROUPS)
    for step in range(n_groups):
        g = n_groups - 1 - step

        @pl.when(gi == step)
        def _(g=g, step=step):
            run_group(A_GROUPS[g][1], step == 0, step == n_groups - 1)


def _attn_prompt(p3, tables):
    bsz, seq, _ = p3.shape
    nblk = seq // ATT_TB
    n_groups = len(A_GROUPS)
    hp_per_group = A_GROUP_WIDTH // LANES

    def col(base):
        return lambda b, i, hp, gi: (b, i, base // LANES + (n_groups - 1 - gi) * hp_per_group + hp)

    def col_prev(base):
        return lambda b, i, hp, gi: (b, jnp.maximum(i - 1, 0),
                                     base // LANES + (n_groups - 1 - gi) * hp_per_group + hp)

    blk = (None, ATT_TB, LANES)
    return pl.pallas_call(
        _attn_prompt_kernel,
        grid=(bsz, nblk, hp_per_group, n_groups),
        in_specs=[pl.BlockSpec(blk, col(COL_QA)),
                  pl.BlockSpec(blk, col(COL_KA)),
                  pl.BlockSpec(blk, col_prev(COL_KA)),
                  pl.BlockSpec(blk, col(COL_VA)),
                  pl.BlockSpec(blk, col_prev(COL_VA)),
                  pl.BlockSpec((None, 2, 2, A_KEYS, 2 * A_KEYS),
                               lambda b, i, hp, gi: (n_groups - 1 - gi, 0, hp, 0, 0))],
        out_specs=pl.BlockSpec(blk, lambda b, i, hp, gi: (b, i, hp)),
        out_shape=jax.ShapeDtypeStruct((bsz, seq, A_GROUP_WIDTH), F32),
        scratch_shapes=[pltpu.VMEM((2 * ATT_TB, LANES), F32),
                        pltpu.VMEM((2 * ATT_TB, LANES), F32),
                        pltpu.VMEM((ATT_TB, LANES), F32),
                        pltpu.VMEM((ATT_TB, LANES), F32),
                        pltpu.VMEM((ATT_TB, LANES), F32)],
        compiler_params=pltpu.CompilerParams(
            dimension_semantics=("arbitrary",) * 4, vmem_limit_bytes=VMEM_LIMIT),
        name="attn_prompt",
    )(p3, p3, p3, p3, p3, tables)


ROLL_ELEMS = 64 * SUBLANES * LANES


def _attn_sample_kernel(qkv_ref, c0_ref, c1_ref, c2_ref, tc0_ref, tc1_ref, tc2_ref, tn_ref,
                        o_ref, n0_ref, n1_ref, n2_ref, nt_ref, *, n_new):
    caches = (c0_ref, c1_ref, c2_ref)
    tables = (tc0_ref, tc1_ref, tc2_ref)
    outs = (n0_ref, n1_ref, n2_ref)
    gw = A_GROUP_WIDTH
    scale = A_HEAD_DIM ** -0.5

    head_of_lane = lax.broadcasted_iota(jnp.int32, (A_HEADS, gw), 1) >> int(math.log2(A_HEAD_DIM))
    head_of_row = lax.broadcasted_iota(jnp.int32, (A_HEADS, gw), 0)
    hmask = head_of_lane == head_of_row

    parts = []
    for g, (window, d) in enumerate(A_GROUPS):
        c_ref, n_ref = caches[g], outs[g]
        q_new = qkv_ref[:, COL_QA + g * gw:COL_QA + (g + 1) * gw]
        k_new = qkv_ref[:, COL_KA + g * gw:COL_KA + (g + 1) * gw]
        v_new = qkv_ref[:, COL_VA + g * gw:COL_VA + (g + 1) * gw]

        qbd = jnp.concatenate([jnp.where(hmask, q_new[i:i + 1, :], 0.0) for i in range(n_new)], axis=0)
        qb = qbd.astype(BF16)
        s_c = jnp.dot(qb, c_ref[0:gw, :].astype(BF16), preferred_element_type=F32) * scale + tables[g][...]
        qf = qb.astype(F32)
        kf = k_new.astype(BF16).astype(F32)
        vf = v_new.astype(BF16).astype(F32)
        s_n = [jnp.sum(qf * kf[j:j + 1, :], axis=-1, keepdims=True) * scale + tn_ref[g, :, j:j + 1]
               for j in range(n_new)]
        mx = functools.reduce(jnp.maximum, s_n, jnp.max(s_c, axis=-1, keepdims=True))
        p_c = jnp.exp(s_c - mx)
        p_n = [jnp.exp(s - mx) for s in s_n]
        ls = jnp.sum(p_c, axis=-1, keepdims=True) + sum(p_n)
        o = lax.dot_general(p_c.astype(BF16), c_ref[gw:2 * gw, :].astype(BF16), (((1,), (1,)), ((), ())),
                            preferred_element_type=F32)
        for j in range(n_new):
            o = o + p_n[j].astype(BF16).astype(F32) * vf[j:j + 1, :]
        parts.append((o, mx, ls))

        for cb in range(2 * gw // LANES):
            src = k_new if cb < gw // LANES else v_new
            c0 = (cb % (gw // LANES)) * LANES
            blk = jnp.concatenate([src[:, c0:c0 + LANES], jnp.zeros((LANES - src.shape[0], LANES), F32)], axis=0)
            nt_ref[cb * LANES:(cb + 1) * LANES, :] = pltpu.roll(blk.T, LANES - n_new, axis=1)

        nrows = min(2 * gw, ROLL_ELEMS // window)
        is_new = lax.broadcasted_iota(jnp.int32, (nrows, LANES), 1) >= LANES - n_new
        for r0 in range(0, 2 * gw, nrows):
            rows = slice(r0, r0 + nrows)
            y = pltpu.roll(c_ref[rows, :], window - n_new, axis=1)
            if window > LANES:
                n_ref[rows, 0:window - LANES] = y[:, 0:window - LANES]
            n_ref[rows, window - LANES:window] = jnp.where(is_new, nt_ref[rows, :], y[:, window - LANES:window])

    mt = functools.reduce(jnp.maximum, [p[1] for p in parts])
    num = sum(jnp.exp(p[1] - mt) * p[0] for p in parts)
    den = sum(jnp.exp(p[1] - mt) * p[2] for p in parts)
    comb = num / den
    for i in range(n_new):
        sel = jnp.where(hmask, comb[i * A_HEADS:(i + 1) * A_HEADS, :], 0.0)
        o_ref[i:i + 1, :] = jnp.sum(sel, axis=0, keepdims=True)


def _attn_sample(qkv, n_new, c0, c1, c2, tabs_c, tab_n):
    dbsz, rows, qkv_w = qkv.shape
    nq = n_new * A_HEADS

    def cspec(w):
        return pl.BlockSpec((None, 2 * A_GROUP_WIDTH, w), lambda b: (b, 0, 0))

    ws = [g[0] for g in A_GROUPS]
    return pl.pallas_call(
        functools.partial(_attn_sample_kernel, n_new=n_new),
        grid=(dbsz,),
        in_specs=[pl.BlockSpec((None, rows, qkv_w), lambda b: (b, 0, 0)),
                  cspec(ws[0]), cspec(ws[1]), cspec(ws[2])]
        + [pl.BlockSpec((nq, w), lambda b: (0, 0)) for w in ws]
        + [pl.BlockSpec((len(A_GROUPS), nq, LANES), lambda b: (0, 0, 0))],
        out_specs=[pl.BlockSpec((None, n_new, A_GROUP_WIDTH), lambda b: (b, 0, 0)),
                   cspec(ws[0]), cspec(ws[1]), cspec(ws[2])],
        out_shape=[jax.ShapeDtypeStruct((dbsz, n_new, A_GROUP_WIDTH), F32)]
        + [jax.ShapeDtypeStruct(c.shape, c.dtype) for c in (c0, c1, c2)],
        scratch_shapes=[pltpu.VMEM((2 * A_GROUP_WIDTH, LANES), F32)],
        compiler_params=pltpu.CompilerParams(
            dimension_semantics=("arbitrary",), vmem_limit_bytes=VMEM_LIMIT),
        name="attn_sample",
    )(qkv, c0, c1, c2, *tabs_c, tab_n)


def _delta_kernel(xq_ref, xk_ref, xv_ref, ab_ref, hist_ref, s0_ref, cw_ref, gp_ref, o_ref, st_ref, xs_ref,
                  *, n_valid):
    c = pl.program_id(1)
    cs = xq_ref.shape[0]
    hd = B_HEAD_DIM

    @pl.when(c == 0)
    def _():
        xs_ref[0:SUBLANES, :] = hist_ref[...]
        st_ref[...] = s0_ref[...]

    for part, ref in enumerate((xq_ref, xk_ref, xv_ref)):
        xs_ref[SUBLANES:SUBLANES + cs, part * B_WIDTH:(part + 1) * B_WIDTH] = ref[...]

    def conv_silu(col0):
        cols = slice(col0, col0 + hd)
        y = xs_ref[pl.ds(SUBLANES - 3, cs), cols] * cw_ref[0:1, cols]
        for t in range(1, CONV_W):
            y = y + xs_ref[pl.ds(SUBLANES - 3 + t, cs), cols] * cw_ref[t:t + 1, cols]
        return _silu(y)

    def l2norm(t):
        return t * lax.rsqrt(jnp.sum(t * t, axis=-1, keepdims=True) + NORM_EPS)

    ab = ab_ref[...]
    g_all = -jnp.exp(gp_ref[0:1, :]) * _softplus(ab + gp_ref[1:2, :])
    beta_all = _sigmoid(ab)
    if n_valid < cs:
        live = lax.broadcasted_iota(jnp.int32, (cs, LANES), 0) < n_valid
        g_all = jnp.where(live, g_all, 0.0)
        beta_all = jnp.where(live, beta_all, 0.0)
    ri = lax.broadcasted_iota(jnp.int32, (cs, cs), 0)
    ci = lax.broadcasted_iota(jnp.int32, (cs, cs), 1)
    causal = ri >= ci
    strict = ri > ci
    eye = (ri == ci).astype(F32)
    cum = jnp.dot(causal.astype(F32), g_all, precision=HIGHEST, preferred_element_type=F32)
    cum_sq = jnp.concatenate([cum, jnp.zeros((LANES - cs, LANES), F32)], axis=0) if cs < LANES else cum
    cum_t = cum_sq.T
    ecum = jnp.exp(cum)
    cum_last = cum[cs - 1:cs, :]
    tail = jnp.exp(cum_last - cum)
    elast = jnp.exp(cum_last)

    heads = range(B_HEADS)
    nt_dims = (((1,), (1,)), ((), ()))

    ks, lows, rhss, decays, qbs, qes, kts = [], [], [], [], [], [], []
    for h in heads:
        qh = l2norm(conv_silu(h * hd)) * (hd ** -0.5)
        kh = l2norm(conv_silu(B_WIDTH + h * hd))
        vh = conv_silu(2 * B_WIDTH + h * hd)
        beta = beta_all[:, SUBLANES + h:SUBLANES + h + 1]
        ec = ecum[:, h:h + 1]
        decay = jnp.exp(jnp.where(causal, cum[:, h:h + 1] - cum_t[h:h + 1, 0:cs], NEG))
        kb = kh.astype(BF16)
        kk = lax.dot_general(kb, kb, nt_dims, preferred_element_type=F32)
        ks.append(kb)
        lows.append(-jnp.where(strict, beta * kk * decay, 0.0))
        rhss.append(jnp.concatenate([beta * vh, beta * kh * ec], axis=1))
        decays.append(decay)
        qbs.append(qh.astype(BF16))
        qes.append((qh * ec).astype(BF16))
        kts.append((kh * tail[:, h:h + 1]).astype(BF16))

    tinvs = [eye + nl for nl in lows]
    pws = lows
    for _ in range(int(math.log2(cs)) - 1):
        pws = [_dot_split(p, p) for p in pws]
        tinvs = [t + _dot_split(t, p) for t, p in zip(tinvs, pws)]
    sols = [_dot_split(t, r) for t, r in zip(tinvs, rhss)]

    for h in heads:
        u = sols[h][:, 0:hd]
        w = sols[h][:, hd:2 * hd]
        state = st_ref[h]
        sb = state.astype(BF16)
        v_new = u - jnp.dot(w.astype(BF16), sb, preferred_element_type=F32)
        vb = v_new.astype(BF16)
        qk = lax.dot_general(qbs[h], ks[h], nt_dims, preferred_element_type=F32) * decays[h]
        out = jnp.dot(qes[h], sb, preferred_element_type=F32)
        out = out + jnp.dot(qk.astype(BF16), vb, preferred_element_type=F32)
        o_ref[:, h * hd:(h + 1) * hd] = out
        st_ref[h] = elast[:, h:h + 1] * state + lax.dot_general(
            kts[h], vb, (((0,), (0,)), ((), ())), preferred_element_type=F32)

    xs_ref[0:SUBLANES, :] = xs_ref[cs:cs + SUBLANES, :]


def _delta(p3, col_qkv, col_ab, hist, state0, conv_w, gate_par, cs, n_valid):
    nb, t, _ = p3.shape
    n_chunks = t // cs
    qkv_w = 3 * B_WIDTH

    def xspec(part):
        return pl.BlockSpec((None, cs, B_WIDTH), lambda b, c: (b, c, col_qkv // B_WIDTH + part))

    return pl.pallas_call(
        functools.partial(_delta_kernel, n_valid=n_valid),
        grid=(nb, n_chunks),
        in_specs=[xspec(0), xspec(1), xspec(2),
                  pl.BlockSpec((None, cs, LANES), lambda b, c: (b, c, col_ab // LANES)),
                  pl.BlockSpec((None, SUBLANES, qkv_w), lambda b, c: (b, 0, 0)),
                  pl.BlockSpec((None, B_HEADS, B_HEAD_DIM, B_HEAD_DIM), lambda b, c: (b, 0, 0, 0)),
                  pl.BlockSpec((CONV_W, qkv_w), lambda b, c: (0, 0)),
                  pl.BlockSpec((SUBLANES, LANES), lambda b, c: (0, 0))],
        out_specs=[pl.BlockSpec((None, cs, B_WIDTH), lambda b, c: (b, c, 0)),
                   pl.BlockSpec((None, B_HEADS, B_HEAD_DIM, B_HEAD_DIM), lambda b, c: (b, 0, 0, 0))],
        out_shape=[jax.ShapeDtypeStruct((nb, t, B_WIDTH), F32),
                   jax.ShapeDtypeStruct((nb, B_HEADS, B_HEAD_DIM, B_HEAD_DIM), F32)],
        scratch_shapes=[pltpu.VMEM((SUBLANES + cs, qkv_w), F32)],
        compiler_params=pltpu.CompilerParams(
            dimension_semantics=("arbitrary", "arbitrary"), vmem_limit_bytes=VMEM_LIMIT),
        name="delta",
    )(p3, p3, p3, p3, hist, state0, conv_w, gate_par)


def _back_kernel(x_ref, gate_ref, oa_ref, za_ref, ob_ref, zb_ref, ga_ref, gb_ref,
                 wa_ref, wb_ref, wo_ref, bn_ref, lg_ref, lb_ref, y_ref):
    ya = (oa_ref[...] * _silu(za_ref[...])).astype(BF16)
    parts = []
    for h in range(B_HEADS):
        cols = slice(h * B_HEAD_DIM, (h + 1) * B_HEAD_DIM)
        t = ob_ref[:, cols]
        t = t * lax.rsqrt(jnp.mean(t * t, axis=-1, keepdims=True) + NORM_EPS) * bn_ref[...]
        parts.append((t * _silu(zb_ref[:, cols])).astype(BF16))
    yb = jnp.concatenate(parts, axis=1)
    ma = jnp.dot(ya, wa_ref[...], preferred_element_type=F32)
    mb = jnp.dot(yb, wb_ref[...], preferred_element_type=F32)
    merged = _sigmoid(ga_ref[...]) * ma + _sigmoid(gb_ref[...]) * mb
    r = ALPHA * x_ref[...] + gate_ref[...] * jnp.dot(merged.astype(BF16), wo_ref[...], preferred_element_type=F32)
    mu = jnp.mean(r, axis=-1, keepdims=True)
    rc = r - mu
    var = jnp.mean(rc * rc, axis=-1, keepdims=True)
    y_ref[...] = rc * lax.rsqrt(var + LN_EPS) * lg_ref[...] + lb_ref[...]


def _back(x, gate, oa, p, ob, wa, wb, wo, bn, lg, lb, tm):
    m, d = x.shape
    groups, r, _ = gate.shape
    blocks_per_group = (m // tm) // groups

    def const(shape):
        return pl.BlockSpec(shape, lambda i: (0,) * len(shape))

    return pl.pallas_call(
        _back_kernel,
        grid=(m // tm,),
        in_specs=[pl.BlockSpec((tm, d), lambda i: (i, 0)),
                  pl.BlockSpec((None, r, d), lambda i: (i // blocks_per_group, 0, 0)),
                  pl.BlockSpec((tm, A_GROUP_WIDTH), lambda i: (i, 0)),
                  pl.BlockSpec((tm, A_GROUP_WIDTH), lambda i: (i, COL_ZA // A_GROUP_WIDTH)),
                  pl.BlockSpec((tm, B_WIDTH), lambda i: (i, 0)),
                  pl.BlockSpec((tm, B_WIDTH), lambda i: (i, COL_ZB // B_WIDTH)),
                  pl.BlockSpec((tm, d), lambda i: (i, COL_GA // d)),
                  pl.BlockSpec((tm, d), lambda i: (i, COL_GB // d)),
                  const(wa.shape), const(wb.shape), const(wo.shape),
                  const((1, B_HEAD_DIM)), const((1, d)), const((1, d))],
        out_specs=pl.BlockSpec((tm, d), lambda i: (i, 0)),
        out_shape=jax.ShapeDtypeStruct((m, d), F32),
        compiler_params=pltpu.CompilerParams(
            dimension_semantics=("arbitrary",), vmem_limit_bytes=VMEM_LIMIT),
        name="back",
    )(x, gate, oa, p, ob, p, p, p, wa, wb, wo, bn, lg, lb)


def _t5_causal_buckets(dist):
    max_exact = N_BUCKETS // 2
    dist = np.asarray(dist, dtype=np.int64)
    ratio = np.maximum(dist, max_exact) / max_exact
    large = max_exact + (np.log(ratio) / math.log(BUCKET_MAX_DIST / max_exact) * (N_BUCKETS - max_exact)).astype(np.int64)
    return np.where(dist < max_exact, dist, np.minimum(large, N_BUCKETS - 1)).astype(np.int32)


def _group_bias(rel_bias, gi):
    window, dil = A_GROUPS[gi]
    buckets = _t5_causal_buckets(dil * np.arange(window // dil + 1))
    onehot = (buckets[:, None] == np.arange(N_BUCKETS)[None, :]).astype(np.float32)
    sel = jnp.dot(onehot, rel_bias[:, gi * A_HEADS:(gi + 1) * A_HEADS], precision=HIGHEST)
    return sel.T


def _neg(*shape):
    return jnp.full(shape, NEG, F32)


def _prompt_tables(rel_bias):
    n = A_KEYS
    tabs = []
    for gi in range(len(A_GROUPS)):
        b = _group_bias(rel_bias, gi)
        c = jnp.concatenate([_neg(A_HEADS, n), b[:, ::-1], _neg(A_HEADS, n - 1)], axis=1)
        flat = jnp.tile(c[:, 1:], (1, n + 1))[:, 0:n * 3 * n]
        hank = flat.reshape(A_HEADS, n, 3 * n)[:, :, 0:2 * n]
        tab = hank[:, ::-1, :]
        tabs.append(jnp.stack([tab, jnp.concatenate([_neg(A_HEADS, n, n), tab[:, :, n:]], axis=2)]))
    return jnp.stack(tabs)


def _sample_tables(rel_bias, n_new):
    tabs_c, tabs_n = [], []
    for gi, (window, dil) in enumerate(A_GROUPS):
        b = _group_bias(rel_bias, gi)
        rev = b[:, :0:-1]
        if dil > 1:
            rev = jnp.concatenate([rev[:, :, None], _neg(A_HEADS, A_KEYS, dil - 1)], axis=2).reshape(A_HEADS, window)
        rows_c, rows_n = [], []
        for i in range(n_new):
            rows_c.append(jnp.concatenate([_neg(A_HEADS, i), rev[:, 0:window - i]], axis=1) if i else rev)
            cols = [b[:, (i - j) // dil][:, None] if (j <= i and (i - j) % dil == 0) else _neg(A_HEADS, 1)
                    for j in range(n_new)]
            rows_n.append(jnp.concatenate(cols + [_neg(A_HEADS, LANES - n_new)], axis=1))
        tabs_c.append(jnp.concatenate(rows_c, axis=0))
        tabs_n.append(jnp.concatenate(rows_n, axis=0))
    return tabs_c, jnp.stack(tabs_n)


def _regroup_w_in(w_in):
    main = w_in[:, 0:9216]
    ab = w_in[:, 9216:9232]
    gates = w_in[:, 9232:11280]
    pad = jnp.zeros((w_in.shape[0], PROJ_COLS - 11280), w_in.dtype)
    return jnp.concatenate([main, gates, ab, pad], axis=1).astype(BF16)


def _kv_tail(p3, gi, rows):
    gw = A_GROUP_WIDTH
    k = p3[:, -rows:, COL_KA + gi * gw:COL_KA + (gi + 1) * gw]
    v = p3[:, -rows:, COL_VA + gi * gw:COL_VA + (gi + 1) * gw]
    bsz = p3.shape[0]
    return jnp.concatenate([k, v], axis=-1).reshape(bsz, rows, 2, A_HEADS, A_HEAD_DIM)


def kernel(x_prompt, x_sample, c_prompt, c_sample, cache_kv_w128, cache_kv_w512, cache_kv_w2048, state_conv, state_delta, w_cond, b_cond, w_in, rel_bias, conv_w, a_log, dt_bias, b_norm_w, w_branch_a, w_branch_b, w_out, ln_g, ln_b):
    bsz, seq, d = x_prompt.shape
    dbsz, dseq, _ = x_sample.shape
    layer = 0

    c_all = jnp.concatenate([c_prompt, c_sample], axis=0)
    c_all = jnp.pad(c_all, ((0, -(bsz + dbsz) % 16), (0, 0)))
    cond = _cond(c_all, w_cond[layer], b_cond[layer])
    shift, scale, gate = cond[:, 0:d], cond[:, d:2 * d], cond[:, 2 * d:3 * d]

    w_in_r = _regroup_w_in(w_in[layer])
    wa = w_branch_a[layer].astype(BF16)
    wb = w_branch_b[layer].astype(BF16)
    wo = w_out[layer].astype(BF16)
    bn = b_norm_w[layer].reshape(1, B_HEAD_DIM)
    lg = ln_g[layer].reshape(1, d)
    lb = ln_b[layer].reshape(1, d)
    gate_par = jnp.zeros((SUBLANES, LANES), F32)
    gate_par = gate_par.at[0, 0:B_HEADS].set(a_log[layer]).at[1, 0:B_HEADS].set(dt_bias[layer])
    cw = conv_w[layer]

    xp = x_prompt.reshape(bsz * seq, d)
    pp = _proj(xp, scale[0:bsz, None, :], shift[0:bsz, None, :], w_in_r, tm=1024)
    pp3 = pp.reshape(bsz, seq, PROJ_COLS)
    oa_p = _attn_prompt(pp3, _prompt_tables(rel_bias))
    hist0 = jnp.zeros((bsz, SUBLANES, 3 * B_WIDTH), F32)
    st0 = jnp.zeros((bsz, B_HEADS, B_HEAD_DIM, B_HEAD_DIM), F32)
    ob_p, st_p = _delta(pp3, COL_QKVB, COL_AB, hist0, st0, cw, gate_par, B_CHUNK, B_CHUNK)
    y_p = _back(xp, gate[0:bsz, None, :], oa_p.reshape(bsz * seq, A_GROUP_WIDTH), pp,
                ob_p.reshape(bsz * seq, B_WIDTH), wa, wb, wo, bn, lg, lb, tm=256)
    y_prompt = y_p.reshape(bsz, seq, d)
    kv_p = [_kv_tail(pp3, gi, min(w, seq))[None] for gi, (w, _) in enumerate(A_GROUPS)]
    conv_p = pp3[:, seq - (CONV_W - 1):, COL_QKVB:COL_QKVB + 3 * B_WIDTH][None]
    delta_p = st_p[None]

    ms = dbsz * dseq
    xs = x_sample.reshape(ms, d)
    rep = lambda t: jnp.repeat(t[bsz:bsz + dbsz], dseq, axis=0)[None]
    ps = _proj(xs, rep(scale), rep(shift), w_in_r, tm=ms)
    ps3 = ps.reshape(dbsz, dseq, PROJ_COLS)
    cache_in = (cache_kv_w128, cache_kv_w512, cache_kv_w2048)
    caches = [jnp.transpose(c[layer], (0, 2, 3, 4, 1)).reshape(dbsz, 2 * A_GROUP_WIDTH, c.shape[2]) for c in cache_in]
    qkv_s = jnp.pad(ps3[:, :, 0:COL_ZA], ((0, 0), (0, 16 - dseq), (0, 0)))
    tabs_c, tab_n = _sample_tables(rel_bias, dseq)
    oa_s, n0, n1, n2 = _attn_sample(qkv_s, dseq, caches[0], caches[1], caches[2], tabs_c, tab_n)
    kv_s = [jnp.transpose(n.reshape(dbsz, 2, A_HEADS, A_HEAD_DIM, n.shape[2]), (0, 4, 1, 2, 3))[None]
            for n in (n0, n1, n2)]
    cs_s = B_CHUNK
    pd = jnp.concatenate([ps3[:, :, COL_QKVB:COL_QKVB + 3 * B_WIDTH], ps3[:, :, COL_AB:COL_AB + LANES]], axis=-1)
    pd = jnp.pad(pd, ((0, 0), (0, cs_s - dseq), (0, 0)))
    hist_s = jnp.pad(state_conv[layer], ((0, 0), (SUBLANES - (CONV_W - 1), 0), (0, 0)))
    ob_s, st_s = _delta(pd, 0, 3 * B_WIDTH, hist_s, state_delta[layer], cw, gate_par, cs_s, dseq)
    y_s = _back(xs, rep(gate), oa_s.reshape(ms, A_GROUP_WIDTH), ps,
                ob_s[:, 0:dseq].reshape(ms, B_WIDTH), wa, wb, wo, bn, lg, lb, tm=ms)
    y_sample = y_s.reshape(dbsz, dseq, d)
    conv_s = ps3[:, dseq - (CONV_W - 1):, COL_QKVB:COL_QKVB + 3 * B_WIDTH][None]
    delta_s = st_s[None]

    return (y_prompt, y_sample, kv_p[0], kv_p[1], kv_p[2], conv_p, delta_p,
            kv_s[0], kv_s[1], kv_s[2], conv_s, delta_s)
```

```python
import functools
import math

import numpy as np
import jax
import jax.numpy as jnp
from jax import lax
from jax.experimental import pallas as pl
from jax.experimental.pallas import tpu as pltpu

F32 = jnp.float32
BF16 = jnp.bfloat16
HIGHEST = lax.Precision.HIGHEST

D_MODEL = 1024
A_GROUPS = ((128, 1), (512, 4), (2048, 16))
A_HEADS = 8
A_HEAD_DIM = 64
A_GROUP_WIDTH = A_HEADS * A_HEAD_DIM
A_KEYS = 128
N_BUCKETS = 32
BUCKET_MAX_DIST = 2048
B_HEADS = 8
B_HEAD_DIM = 128
B_WIDTH = B_HEADS * B_HEAD_DIM
CONV_W = 4
B_CHUNK = 64
DEPTH = 1
ALPHA = (2 * DEPTH) ** 0.25
LN_EPS = 1e-5
NORM_EPS = 1e-6
NEG = -1e30

LANES = 128
SUBLANES = 8

COL_QA = 0
COL_KA = 1536
COL_VA = 3072
COL_ZA = 4608
COL_QKVB = 5120
COL_ZB = 8192
COL_GA = 9216
COL_GB = 10240
COL_AB = 11264
PROJ_COLS = 11520
QKVB_COLS = 3 * B_WIDTH
PROJ_TN = 1280

VMEM_LIMIT = 48 * 1024 * 1024


def _sigmoid(x):
    return 1.0 / (1.0 + jnp.exp(-x))


def _silu(x):
    return x * _sigmoid(x)


def _softplus(x):
    return jnp.maximum(x, 0.0) + jnp.log1p(jnp.exp(-jnp.abs(x)))


def _split(x):
    hi = x.astype(BF16)
    return hi, (x - hi.astype(F32)).astype(BF16)


def _dot_split(a, b):
    lhs = jnp.concatenate([a[0], a[1], a[0]], axis=1)
    rhs = jnp.concatenate([b[0], b[0], b[1]], axis=0)
    return jnp.dot(lhs, rhs, preferred_element_type=F32)


def _cond_kernel(c_ref, w_ref, b_ref, o_ref):
    s = _silu(c_ref[...]).astype(BF16)
    o_ref[...] = jnp.dot(s, w_ref[...].astype(BF16), preferred_element_type=F32) + b_ref[...]


def _cond(c, w, b):
    n, d = c.shape
    width = w.shape[1]
    tn = 512
    return pl.pallas_call(
        _cond_kernel,
        grid=(width // tn,),
        in_specs=[pl.BlockSpec((n, d), lambda j: (0, 0)),
                  pl.BlockSpec((d, tn), lambda j: (0, j)),
                  pl.BlockSpec((1, tn), lambda j: (0, j))],
        out_specs=pl.BlockSpec((n, tn), lambda j: (0, j)),
        out_shape=jax.ShapeDtypeStruct((n, width), F32),
        name="cond",
    )(c, w, b.reshape(1, width))


def _proj_kernel(x_ref, sc_ref, sh_ref, w_ref, o_ref, h_ref):
    @pl.when(pl.program_id(1) == 0)
    def _():
        h_ref[...] = (x_ref[...] * (1.0 + sc_ref[...]) + sh_ref[...]).astype(BF16)

    o_ref[...] = jnp.dot(h_ref[...], w_ref[...], preferred_element_type=F32)


def _proj(x, scale, shift, w, tm):
    m, d = x.shape
    n = w.shape[1]
    groups, r, _ = scale.shape
    blocks_per_group = (m // tm) // groups
    mod_spec = pl.BlockSpec((None, r, d), lambda i, j: (i // blocks_per_group, 0, 0))
    return pl.pallas_call(
        _proj_kernel,
        grid=(m // tm, n // PROJ_TN),
        in_specs=[pl.BlockSpec((tm, d), lambda i, j: (i, 0)),
                  mod_spec, mod_spec,
                  pl.BlockSpec((d, PROJ_TN), lambda i, j: (0, j))],
        out_specs=pl.BlockSpec((tm, PROJ_TN), lambda i, j: (i, j)),
        out_shape=jax.ShapeDtypeStruct((m, n), F32),
        scratch_shapes=[pltpu.VMEM((tm, d), BF16)],
        compiler_params=pltpu.CompilerParams(
            dimension_semantics=("arbitrary", "arbitrary"), vmem_limit_bytes=VMEM_LIMIT),
        name="proj",
    )(x, scale, shift, w)


ATT_TB = 2048


def _attn_prompt_kernel(q_ref, k_ref, kp_ref, v_ref, vp_ref, tb_ref, o_ref,
                        kc_ref, vc_ref, acc_ref, m_ref, l_ref):
    i = pl.program_id(1)
    gi = pl.program_id(3)
    tb = ATT_TB
    kc_ref[0:tb, :] = kp_ref[...]
    kc_ref[tb:2 * tb, :] = k_ref[...]
    vc_ref[0:tb, :] = vp_ref[...]
    vc_ref[tb:2 * tb, :] = v_ref[...]

    lane = lax.broadcasted_iota(jnp.int32, (A_KEYS, LANES), 1)
    lo = lane < A_HEAD_DIM

    def run_group(d, first, last):
        tq = A_KEYS * d
        shift = int(math.log2(d))

        def body(u, carry):
            sb = u >> shift
            r = u & (d - 1)
            start = sb * tq + r
            rows = pl.ds(start, A_KEYS, stride=d)
            q = q_ref[rows, :]
            krows = pl.ds(tb - tq + start, 2 * A_KEYS, stride=d)
            kk = kc_ref[krows, :].astype(BF16)
            vv = vc_ref[krows, :].astype(BF16)
            variant = jnp.logical_and(i == 0, sb == 0).astype(jnp.int32)
            res = []
            for hh in range(2):
                sel = lo if hh == 0 else jnp.logical_not(lo)
                qh = jnp.where(sel, q * (A_HEAD_DIM ** -0.5), 0.0).astype(BF16)
                s = lax.dot_general(qh, kk, (((1,), (1,)), ((), ())), preferred_element_type=F32)
                s = s + tb_ref[variant, hh]
                mx = jnp.max(s, axis=-1, keepdims=True)
                p = jnp.exp(s - mx)
                ls = jnp.sum(p, axis=-1, keepdims=True)
                o = jnp.dot(p.astype(BF16), vv, preferred_element_type=F32)
                res.append((o, mx, ls))
            o = jnp.where(lo, res[0][0], res[1][0])
            mx = jnp.where(lo, res[0][1], res[1][1])
            ls = jnp.where(lo, res[0][2], res[1][2])
            if first:
                acc_ref[rows, :] = o
                m_ref[rows, :] = mx
                l_ref[rows, :] = ls
            else:
                mo = m_ref[rows, :]
                mn = jnp.maximum(mo, mx)
                a = jnp.exp(mo - mn)
                b = jnp.exp(mx - mn)
                acc = a * acc_ref[rows, :] + b * o
                ll = a * l_ref[rows, :] + b * ls
                if last:
                    o_ref[rows, :] = acc / ll
                else:
                    acc_ref[rows, :] = acc
                    m_ref[rows, :] = mn
                    l_ref[rows, :] = ll
            return carry

        lax.fori_loop(0, tb // A_KEYS, body, 0, unroll=4)

    n_groups = len(A_GROUPS)
    for step in range(n_groups):
        g = n_groups - 1 - step

        @pl.when(gi == step)
        def _(g=g, step=step):
            run_group(A_GROUPS[g][1], step == 0, step == n_groups - 1)


def _attn_prompt(p3, tables):
    bsz, seq, _ = p3.shape
    nblk = seq // ATT_TB
    n_groups = len(A_GROUPS)
    hp_per_group = A_GROUP_WIDTH // LANES

    def col(base):
        return lambda b, i, hp, gi: (b, i, base // LANES + (n_groups - 1 - gi) * hp_per_group + hp)

    def col_prev(base):
        return lambda b, i, hp, gi: (b, jnp.maximum(i - 1, 0),
                                     base // LANES + (n_groups - 1 - gi) * hp_per_group + hp)

    blk = (None, ATT_TB, LANES)
    return pl.pallas_call(
        _attn_prompt_kernel,
        grid=(bsz, nblk, hp_per_group, n_groups),
        in_specs=[pl.BlockSpec(blk, col(COL_QA)),
                  pl.BlockSpec(blk, col(COL_KA)),
                  pl.BlockSpec(blk, col_prev(COL_KA)),
                  pl.BlockSpec(blk, col(COL_VA)),
                  pl.BlockSpec(blk, col_prev(COL_VA)),
                  pl.BlockSpec((None, 2, 2, A_KEYS, 2 * A_KEYS),
                               lambda b, i, hp, gi: (n_groups - 1 - gi, 0, hp, 0, 0))],
        out_specs=pl.BlockSpec(blk, lambda b, i, hp, gi: (b, i, hp)),
        out_shape=jax.ShapeDtypeStruct((bsz, seq, A_GROUP_WIDTH), F32),
        scratch_shapes=[pltpu.VMEM((2 * ATT_TB, LANES), F32),
                        pltpu.VMEM((2 * ATT_TB, LANES), F32),
                        pltpu.VMEM((ATT_TB, LANES), F32),
                        pltpu.VMEM((ATT_TB, LANES), F32),
                        pltpu.VMEM((ATT_TB, LANES), F32)],
        compiler_params=pltpu.CompilerParams(
            dimension_semantics=("arbitrary",) * 4, vmem_limit_bytes=VMEM_LIMIT),
        name="attn_prompt",
    )(p3, p3, p3, p3, p3, tables)


ROLL_ELEMS = 64 * SUBLANES * LANES


def _attn_sample_kernel(qkv_ref, c0_ref, c1_ref, c2_ref, tc0_ref, tc1_ref, tc2_ref, tn_ref,
                        o_ref, n0_ref, n1_ref, n2_ref, nt_ref, *, n_new):
    caches = (c0_ref, c1_ref, c2_ref)
    tables = (tc0_ref, tc1_ref, tc2_ref)
    outs = (n0_ref, n1_ref, n2_ref)
    gw = A_GROUP_WIDTH
    scale = A_HEAD_DIM ** -0.5

    head_of_lane = lax.broadcasted_iota(jnp.int32, (A_HEADS, gw), 1) >> int(math.log2(A_HEAD_DIM))
    head_of_row = lax.broadcasted_iota(jnp.int32, (A_HEADS, gw), 0)
    hmask = head_of_lane == head_of_row

    parts = []
    for g, (window, d) in enumerate(A_GROUPS):
        c_ref, n_ref = caches[g], outs[g]
        q_new = qkv_ref[:, COL_QA + g * gw:COL_QA + (g + 1) * gw]
        k_new = qkv_ref[:, COL_KA + g * gw:COL_KA + (g + 1) * gw]
        v_new = qkv_ref[:, COL_VA + g * gw:COL_VA + (g + 1) * gw]

        qbd = jnp.concatenate([jnp.where(hmask, q_new[i:i + 1, :], 0.0) for i in range(n_new)], axis=0)
        qb = qbd.astype(BF16)
        s_c = jnp.dot(qb, c_ref[0:gw, :].astype(BF16), preferred_element_type=F32) * scale + tables[g][...]
        qf = qb.astype(F32)
        kf = k_new.astype(BF16).astype(F32)
        vf = v_new.astype(BF16).astype(F32)
        s_n = [jnp.sum(qf * kf[j:j + 1, :], axis=-1, keepdims=True) * scale + tn_ref[g, :, j:j + 1]
               for j in range(n_new)]
        mx = functools.reduce(jnp.maximum, s_n, jnp.max(s_c, axis=-1, keepdims=True))
        p_c = jnp.exp(s_c - mx)
        p_n = [jnp.exp(s - mx) for s in s_n]
        ls = jnp.sum(p_c, axis=-1, keepdims=True) + sum(p_n)
        o = lax.dot_general(p_c.astype(BF16), c_ref[gw:2 * gw, :].astype(BF16), (((1,), (1,)), ((), ())),
                            preferred_element_type=F32)
        for j in range(n_new):
            o = o + p_n[j].astype(BF16).astype(F32) * vf[j:j + 1, :]
        parts.append((o, mx, ls))

        for cb in range(2 * gw // LANES):
            src = k_new if cb < gw // LANES else v_new
            c0 = (cb % (gw // LANES)) * LANES
            blk = jnp.concatenate([src[:, c0:c0 + LANES], jnp.zeros((LANES - src.shape[0], LANES), F32)], axis=0)
            nt_ref[cb * LANES:(cb + 1) * LANES, :] = pltpu.roll(blk.T, LANES - n_new, axis=1)

        nrows = min(2 * gw, ROLL_ELEMS // window)
        is_new = lax.broadcasted_iota(jnp.int32, (nrows, LANES), 1) >= LANES - n_new
        for r0 in range(0, 2 * gw, nrows):
            rows = slice(r0, r0 + nrows)
            y = pltpu.roll(c_ref[rows, :], window - n_new, axis=1)
            if window > LANES:
                n_ref[rows, 0:window - LANES] = y[:, 0:window - LANES]
            n_ref[rows, window - LANES:window] = jnp.where(is_new, nt_ref[rows, :], y[:, window - LANES:window])

    mt = functools.reduce(jnp.maximum, [p[1] for p in parts])
    num = sum(jnp.exp(p[1] - mt) * p[0] for p in parts)
    den = sum(jnp.exp(p[1] - mt) * p[2] for p in parts)
    comb = num / den
    for i in range(n_new):
        sel = jnp.where(hmask, comb[i * A_HEADS:(i + 1) * A_HEADS, :], 0.0)
        o_ref[i:i + 1, :] = jnp.sum(sel, axis=0, keepdims=True)


def _attn_sample(qkv, n_new, c0, c1, c2, tabs_c, tab_n):
    dbsz, rows, qkv_w = qkv.shape
    nq = n_new * A_HEADS

    def cspec(w):
        return pl.BlockSpec((None, 2 * A_GROUP_WIDTH, w), lambda b: (b, 0, 0))

    ws = [g[0] for g in A_GROUPS]
    return pl.pallas_call(
        functools.partial(_attn_sample_kernel, n_new=n_new),
        grid=(dbsz,),
        in_specs=[pl.BlockSpec((None, rows, qkv_w), lambda b: (b, 0, 0)),
                  cspec(ws[0]), cspec(ws[1]), cspec(ws[2])]
        + [pl.BlockSpec((nq, w), lambda b: (0, 0)) for w in ws]
        + [pl.BlockSpec((len(A_GROUPS), nq, LANES), lambda b: (0, 0, 0))],
        out_specs=[pl.BlockSpec((None, n_new, A_GROUP_WIDTH), lambda b: (b, 0, 0)),
                   cspec(ws[0]), cspec(ws[1]), cspec(ws[2])],
        out_shape=[jax.ShapeDtypeStruct((dbsz, n_new, A_GROUP_WIDTH), F32)]
        + [jax.ShapeDtypeStruct(c.shape, c.dtype) for c in (c0, c1, c2)],
        scratch_shapes=[pltpu.VMEM((2 * A_GROUP_WIDTH, LANES), F32)],
        compiler_params=pltpu.CompilerParams(
            dimension_semantics=("arbitrary",), vmem_limit_bytes=VMEM_LIMIT),
        name="attn_sample",
    )(qkv, c0, c1, c2, *tabs_c, tab_n)


def _delta_kernel(xq_ref, xk_ref, xv_ref, ab_ref, hist_ref, s0_ref, cw_ref, gp_ref, o_ref, st_ref, xs_ref,
                  *, cs, n_valid):
    c = pl.program_id(1)
    n_sub = xq_ref.shape[0] // cs
    hd = B_HEAD_DIM
    heads = range(B_HEADS)
    nt_dims = (((1,), (1,)), ((), ()))

    @pl.when(c == 0)
    def _():
        st_ref[...] = s0_ref[...]
        xs_ref[0:SUBLANES, :] = hist_ref[...]

    for part, ref in enumerate((xq_ref, xk_ref, xv_ref)):
        xs_ref[SUBLANES:SUBLANES + n_sub * cs, part * B_WIDTH:(part + 1) * B_WIDTH] = ref[...]

    def activated(row0, part, h):
        cols = slice(part * B_WIDTH + h * hd, part * B_WIDTH + (h + 1) * hd)
        first = row0 + SUBLANES - (CONV_W - 1)
        y = xs_ref[pl.ds(first, cs), cols] * cw_ref[0:1, cols]
        for t in range(1, CONV_W):
            y = y + xs_ref[pl.ds(first + t, cs), cols] * cw_ref[t:t + 1, cols]
        return _silu(y)

    def l2norm(t):
        return t * lax.rsqrt(jnp.sum(t * t, axis=-1, keepdims=True) + NORM_EPS)

    ri = lax.broadcasted_iota(jnp.int32, (cs, cs), 0)
    ci = lax.broadcasted_iota(jnp.int32, (cs, cs), 1)
    causal = ri >= ci
    strict = ri > ci
    eye = (ri == ci).astype(F32)

    pre = []
    for sub in range(n_sub):
        row0 = sub * cs
        ab = ab_ref[row0:row0 + cs, :]
        g_all = -jnp.exp(gp_ref[0:1, :]) * _softplus(ab + gp_ref[1:2, :])
        beta_all = _sigmoid(ab)
        if n_valid < cs:
            live = lax.broadcasted_iota(jnp.int32, (cs, LANES), 0) < n_valid
            g_all = jnp.where(live, g_all, 0.0)
            beta_all = jnp.where(live, beta_all, 0.0)
        cum = jnp.dot(causal.astype(F32), g_all, precision=HIGHEST, preferred_element_type=F32)
        cum_sq = jnp.concatenate([cum, jnp.zeros((LANES - cs, LANES), F32)], axis=0) if cs < LANES else cum
        cum_t = cum_sq.T
        ecum = jnp.exp(cum)
        cum_last = cum[cs - 1:cs, :]
        tail = jnp.exp(cum_last - cum)
        elast = jnp.exp(cum_last)
        for h in heads:
            qh = l2norm(activated(row0, 0, h)) * (hd ** -0.5)
            kh = l2norm(activated(row0, 1, h))
            vh = activated(row0, 2, h)
            beta = beta_all[:, SUBLANES + h:SUBLANES + h + 1]
            ec = ecum[:, h:h + 1]
            decay = jnp.exp(jnp.where(causal, cum[:, h:h + 1] - cum_t[h:h + 1, 0:cs], NEG))
            kb = kh.astype(BF16)
            kk = lax.dot_general(kb, kb, nt_dims, preferred_element_type=F32)
            pre.append(dict(
                sub=sub, h=h, kb=kb, decay=decay, qb=qh.astype(BF16), qe=(qh * ec).astype(BF16),
                kt=(kh * tail[:, h:h + 1]).astype(BF16), elast=elast[:, h:h + 1],
                low=-jnp.where(strict, beta * kk * decay, 0.0),
                rhs=jnp.concatenate([beta * vh, beta * kh * ec], axis=1)))

    tinvs = [eye + e["low"] for e in pre]
    pbs = [e["low"].astype(BF16) for e in pre]
    for _ in range(int(math.log2(cs)) - 1):
        pbs = [jnp.dot(p, p, preferred_element_type=F32).astype(BF16) for p in pbs]
        tinvs = [t + jnp.dot(t.astype(BF16), p, preferred_element_type=F32) for t, p in zip(tinvs, pbs)]
    tbs = [t.astype(BF16) for t in tinvs]
    sol0 = [jnp.dot(t, e["rhs"].astype(BF16), preferred_element_type=F32) for t, e in zip(tbs, pre)]
    resid = [e["rhs"] - s0 + _dot_split(_split(e["low"]), _split(s0)) for e, s0 in zip(pre, sol0)]
    sols = [s0 + jnp.dot(t, r.astype(BF16), preferred_element_type=F32) for s0, t, r in zip(sol0, tbs, resid)]

    for e, sol in zip(pre, sols):
        h, row0 = e["h"], e["sub"] * cs
        u = sol[:, 0:hd]
        w = sol[:, hd:2 * hd]
        state = st_ref[h]
        sb = state.astype(BF16)
        v_new = u - jnp.dot(w.astype(BF16), sb, preferred_element_type=F32)
        vb = v_new.astype(BF16)
        qk = lax.dot_general(e["qb"], e["kb"], nt_dims, preferred_element_type=F32) * e["decay"]
        out = jnp.dot(e["qe"], sb, preferred_element_type=F32)
        out = out + jnp.dot(qk.astype(BF16), vb, preferred_element_type=F32)
        o_ref[row0:row0 + cs, h * hd:(h + 1) * hd] = out
        st_ref[h] = e["elast"] * state + lax.dot_general(
            e["kt"], vb, (((0,), (0,)), ((), ())), preferred_element_type=F32)

    xs_ref[0:SUBLANES, :] = xs_ref[n_sub * cs:n_sub * cs + SUBLANES, :]


def _delta(p3, col_qkv, col_ab, hist, state0, conv_w, gate_par, cs, n_sub, n_valid):
    nb, t, _ = p3.shape
    rows = cs * n_sub
    qkv_w = 3 * B_WIDTH

    def xspec(part):
        return pl.BlockSpec((None, rows, B_WIDTH), lambda b, c: (b, c, col_qkv // B_WIDTH + part))

    return pl.pallas_call(
        functools.partial(_delta_kernel, cs=cs, n_valid=n_valid),
        grid=(nb, t // rows),
        in_specs=[xspec(0), xspec(1), xspec(2),
                  pl.BlockSpec((None, rows, LANES), lambda b, c: (b, c, col_ab // LANES)),
                  pl.BlockSpec((None, SUBLANES, qkv_w), lambda b, c: (b, 0, 0)),
                  pl.BlockSpec((None, B_HEADS, B_HEAD_DIM, B_HEAD_DIM), lambda b, c: (b, 0, 0, 0)),
                  pl.BlockSpec((CONV_W, qkv_w), lambda b, c: (0, 0)),
                  pl.BlockSpec((SUBLANES, LANES), lambda b, c: (0, 0))],
        out_specs=[pl.BlockSpec((None, rows, B_WIDTH), lambda b, c: (b, c, 0)),
                   pl.BlockSpec((None, B_HEADS, B_HEAD_DIM, B_HEAD_DIM), lambda b, c: (b, 0, 0, 0))],
        out_shape=[jax.ShapeDtypeStruct((nb, t, B_WIDTH), F32),
                   jax.ShapeDtypeStruct((nb, B_HEADS, B_HEAD_DIM, B_HEAD_DIM), F32)],
        scratch_shapes=[pltpu.VMEM((SUBLANES + rows, qkv_w), F32)],
        compiler_params=pltpu.CompilerParams(
            dimension_semantics=("arbitrary", "arbitrary"), vmem_limit_bytes=VMEM_LIMIT),
        name="delta",
    )(p3, p3, p3, p3, hist, state0, conv_w, gate_par)


def _back_kernel(x_ref, gate_ref, oa_ref, za_ref, ob_ref, zb_ref, ga_ref, gb_ref,
                 wa_ref, wb_ref, wo_ref, bn_ref, lg_ref, lb_ref, y_ref):
    ya = (oa_ref[...] * _silu(za_ref[...])).astype(BF16)
    parts = []
    for h in range(B_HEADS):
        cols = slice(h * B_HEAD_DIM, (h + 1) * B_HEAD_DIM)
        t = ob_ref[:, cols]
        t = t * lax.rsqrt(jnp.mean(t * t, axis=-1, keepdims=True) + NORM_EPS) * bn_ref[...]
        parts.append((t * _silu(zb_ref[:, cols])).astype(BF16))
    yb = jnp.concatenate(parts, axis=1)
    ma = jnp.dot(ya, wa_ref[...], preferred_element_type=F32)
    mb = jnp.dot(yb, wb_ref[...], preferred_element_type=F32)
    merged = _sigmoid(ga_ref[...]) * ma + _sigmoid(gb_ref[...]) * mb
    r = ALPHA * x_ref[...] + gate_ref[...] * jnp.dot(merged.astype(BF16), wo_ref[...], preferred_element_type=F32)
    mu = jnp.mean(r, axis=-1, keepdims=True)
    rc = r - mu
    var = jnp.mean(rc * rc, axis=-1, keepdims=True)
    y_ref[...] = rc * lax.rsqrt(var + LN_EPS) * lg_ref[...] + lb_ref[...]


def _back(x, gate, oa, p, ob, wa, wb, wo, bn, lg, lb, tm):
    m, d = x.shape
    groups, r, _ = gate.shape
    blocks_per_group = (m // tm) // groups

    def const(shape):
        return pl.BlockSpec(shape, lambda i: (0,) * len(shape))

    return pl.pallas_call(
        _back_kernel,
        grid=(m // tm,),
        in_specs=[pl.BlockSpec((tm, d), lambda i: (i, 0)),
                  pl.BlockSpec((None, r, d), lambda i: (i // blocks_per_group, 0, 0)),
                  pl.BlockSpec((tm, A_GROUP_WIDTH), lambda i: (i, 0)),
                  pl.BlockSpec((tm, A_GROUP_WIDTH), lambda i: (i, COL_ZA // A_GROUP_WIDTH)),
                  pl.BlockSpec((tm, B_WIDTH), lambda i: (i, 0)),
                  pl.BlockSpec((tm, B_WIDTH), lambda i: (i, COL_ZB // B_WIDTH)),
                  pl.BlockSpec((tm, d), lambda i: (i, COL_GA // d)),
                  pl.BlockSpec((tm, d), lambda i: (i, COL_GB // d)),
                  const(wa.shape), const(wb.shape), const(wo.shape),
                  const((1, B_HEAD_DIM)), const((1, d)), const((1, d))],
        out_specs=pl.BlockSpec((tm, d), lambda i: (i, 0)),
        out_shape=jax.ShapeDtypeStruct((m, d), F32),
        compiler_params=pltpu.CompilerParams(
            dimension_semantics=("arbitrary",), vmem_limit_bytes=VMEM_LIMIT),
        name="back",
    )(x, gate, oa, p, ob, p, p, p, wa, wb, wo, bn, lg, lb)


def _t5_causal_buckets(dist):
    max_exact = N_BUCKETS // 2
    dist = np.asarray(dist, dtype=np.int64)
    ratio = np.maximum(dist, max_exact) / max_exact
    large = max_exact + (np.log(ratio) / math.log(BUCKET_MAX_DIST / max_exact) * (N_BUCKETS - max_exact)).astype(np.int64)
    return np.where(dist < max_exact, dist, np.minimum(large, N_BUCKETS - 1)).astype(np.int32)


def _group_bias(rel_bias, gi):
    window, dil = A_GROUPS[gi]
    buckets = _t5_causal_buckets(dil * np.arange(window // dil + 1))
    onehot = (buckets[:, None] == np.arange(N_BUCKETS)[None, :]).astype(np.float32)
    sel = jnp.dot(onehot, rel_bias[:, gi * A_HEADS:(gi + 1) * A_HEADS], precision=HIGHEST)
    return sel.T


def _neg(*shape):
    return jnp.full(shape, NEG, F32)


def _prompt_tables(rel_bias):
    n = A_KEYS
    tabs = []
    for gi in range(len(A_GROUPS)):
        b = _group_bias(rel_bias, gi)
        c = jnp.concatenate([_neg(A_HEADS, n), b[:, ::-1], _neg(A_HEADS, n - 1)], axis=1)
        flat = jnp.tile(c[:, 1:], (1, n + 1))[:, 0:n * 3 * n]
        hank = flat.reshape(A_HEADS, n, 3 * n)[:, :, 0:2 * n]
        tab = hank[:, ::-1, :]
        tabs.append(jnp.stack([tab, jnp.concatenate([_neg(A_HEADS, n, n), tab[:, :, n:]], axis=2)]))
    return jnp.stack(tabs)


def _sample_tables(rel_bias, n_new):
    tabs_c, tabs_n = [], []
    for gi, (window, dil) in enumerate(A_GROUPS):
        b = _group_bias(rel_bias, gi)
        rev = b[:, :0:-1]
        if dil > 1:
            rev = jnp.concatenate([rev[:, :, None], _neg(A_HEADS, A_KEYS, dil - 1)], axis=2).reshape(A_HEADS, window)
        rows_c, rows_n = [], []
        for i in range(n_new):
            rows_c.append(jnp.concatenate([_neg(A_HEADS, i), rev[:, 0:window - i]], axis=1) if i else rev)
            cols = [b[:, (i - j) // dil][:, None] if (j <= i and (i - j) % dil == 0) else _neg(A_HEADS, 1)
                    for j in range(n_new)]
            rows_n.append(jnp.concatenate(cols + [_neg(A_HEADS, LANES - n_new)], axis=1))
        tabs_c.append(jnp.concatenate(rows_c, axis=0))
        tabs_n.append(jnp.concatenate(rows_n, axis=0))
    return tabs_c, jnp.stack(tabs_n)


def _regroup_w_in(w_in):
    main = w_in[:, 0:9216]
    ab = w_in[:, 9216:9232]
    gates = w_in[:, 9232:11280]
    pad = jnp.zeros((w_in.shape[0], PROJ_COLS - (COL_AB + 2 * B_HEADS)), w_in.dtype)
    return jnp.concatenate([main, gates, ab, pad], axis=1).astype(BF16)


def _kv_tail(p3, gi, rows):
    gw = A_GROUP_WIDTH
    k = p3[:, -rows:, COL_KA + gi * gw:COL_KA + (gi + 1) * gw]
    v = p3[:, -rows:, COL_VA + gi * gw:COL_VA + (gi + 1) * gw]
    bsz = p3.shape[0]
    return jnp.concatenate([k, v], axis=-1).reshape(bsz, rows, 2, A_HEADS, A_HEAD_DIM)


def kernel(x_prompt, x_sample, c_prompt, c_sample, cache_kv_w128, cache_kv_w512, cache_kv_w2048, state_conv, state_delta, w_cond, b_cond, w_in, rel_bias, conv_w, a_log, dt_bias, b_norm_w, w_branch_a, w_branch_b, w_out, ln_g, ln_b):
    bsz, seq, d = x_prompt.shape
    dbsz, dseq, _ = x_sample.shape
    layer = 0

    c_all = jnp.concatenate([c_prompt, c_sample], axis=0)
    c_all = jnp.pad(c_all, ((0, -(bsz + dbsz) % 16), (0, 0)))
    cond = _cond(c_all, w_cond[layer], b_cond[layer])
    shift, scale, gate = cond[:, 0:d], cond[:, d:2 * d], cond[:, 2 * d:3 * d]

    w_in_r = _regroup_w_in(w_in[layer])
    wa = w_branch_a[layer].astype(BF16)
    wb = w_branch_b[layer].astype(BF16)
    wo = w_out[layer].astype(BF16)
    bn = b_norm_w[layer].reshape(1, B_HEAD_DIM)
    lg = ln_g[layer].reshape(1, d)
    lb = ln_b[layer].reshape(1, d)
    gate_par = jnp.zeros((SUBLANES, LANES), F32)
    gate_par = gate_par.at[0, 0:B_HEADS].set(a_log[layer]).at[1, 0:B_HEADS].set(dt_bias[layer])
    cw = conv_w[layer]

    xp = x_prompt.reshape(bsz * seq, d)
    pp = _proj(xp, scale[0:bsz, None, :], shift[0:bsz, None, :], w_in_r, tm=1024)
    pp3 = pp.reshape(bsz, seq, PROJ_COLS)
    oa_p = _attn_prompt(pp3, _prompt_tables(rel_bias))
    hist0 = jnp.zeros((bsz, SUBLANES, QKVB_COLS), F32)
    st0 = jnp.zeros((bsz, B_HEADS, B_HEAD_DIM, B_HEAD_DIM), F32)
    ob_p, st_p = _delta(pp3, COL_QKVB, COL_AB, hist0, st0, cw, gate_par, B_CHUNK, 2, B_CHUNK)
    y_p = _back(xp, gate[0:bsz, None, :], oa_p.reshape(bsz * seq, A_GROUP_WIDTH), pp,
                ob_p.reshape(bsz * seq, B_WIDTH), wa, wb, wo, bn, lg, lb, tm=256)
    y_prompt = y_p.reshape(bsz, seq, d)
    kv_p = [_kv_tail(pp3, gi, min(w, seq))[None] for gi, (w, _) in enumerate(A_GROUPS)]
    conv_p = pp3[:, seq - (CONV_W - 1):, COL_QKVB:COL_QKVB + QKVB_COLS][None]
    delta_p = st_p[None]

    ms = dbsz * dseq
    xs = x_sample.reshape(ms, d)
    rep = lambda t: jnp.repeat(t[bsz:bsz + dbsz], dseq, axis=0)[None]
    ps = _proj(xs, rep(scale), rep(shift), w_in_r, tm=ms)
    ps3 = ps.reshape(dbsz, dseq, PROJ_COLS)
    cache_in = (cache_kv_w128, cache_kv_w512, cache_kv_w2048)
    caches = [jnp.transpose(c[layer], (0, 2, 3, 4, 1)).reshape(dbsz, 2 * A_GROUP_WIDTH, c.shape[2]) for c in cache_in]
    qkv_s = jnp.pad(ps3[:, :, 0:COL_ZA], ((0, 0), (0, 16 - dseq), (0, 0)))
    tabs_c, tab_n = _sample_tables(rel_bias, dseq)
    oa_s, n0, n1, n2 = _attn_sample(qkv_s, dseq, caches[0], caches[1], caches[2], tabs_c, tab_n)
    kv_s = [jnp.transpose(n.reshape(dbsz, 2, A_HEADS, A_HEAD_DIM, n.shape[2]), (0, 4, 1, 2, 3))[None]
            for n in (n0, n1, n2)]
    cs_s = SUBLANES
    pad_rows = ((0, 0), (0, cs_s - dseq), (0, 0))
    hist_s = jnp.pad(state_conv[layer], ((0, 0), (SUBLANES - (CONV_W - 1), 0), (0, 0)))
    pd = jnp.concatenate([ps3[:, :, COL_QKVB:COL_QKVB + QKVB_COLS], ps3[:, :, COL_AB:COL_AB + LANES]], axis=-1)
    ob_s, st_s = _delta(jnp.pad(pd, pad_rows), 0, QKVB_COLS, hist_s, state_delta[layer], cw, gate_par,
                        cs_s, 1, dseq)
    y_s = _back(xs, rep(gate), oa_s.reshape(ms, A_GROUP_WIDTH), ps,
                ob_s[:, 0:dseq].reshape(ms, B_WIDTH), wa, wb, wo, bn, lg, lb, tm=ms)
    y_sample = y_s.reshape(dbsz, dseq, d)
    conv_s = ps3[:, dseq - (CONV_W - 1):, COL_QKVB:COL_QKVB + QKVB_COLS][None]
    delta_s = st_s[None]

    return (y_prompt, y_sample, kv_p[0], kv_p[1], kv_p[2], conv_p, delta_p,
            kv_s[0], kv_s[1], kv_s[2], conv_s, delta_s)
```

```python
import functools
import math

import numpy as np
import jax
import jax.numpy as jnp
from jax import lax
from jax.experimental import pallas as pl
from jax.experimental.pallas import tpu as pltpu

F32 = jnp.float32
BF16 = jnp.bfloat16
HIGHEST = lax.Precision.HIGHEST

D_MODEL = 1024
A_GROUPS = ((128, 1), (512, 4), (2048, 16))
A_HEADS = 8
A_HEAD_DIM = 64
A_GROUP_WIDTH = A_HEADS * A_HEAD_DIM
A_KEYS = 128
N_BUCKETS = 32
BUCKET_MAX_DIST = 2048
B_HEADS = 8
B_HEAD_DIM = 128
B_WIDTH = B_HEADS * B_HEAD_DIM
CONV_W = 4
B_CHUNK = 64
DEPTH = 1
ALPHA = (2 * DEPTH) ** 0.25
LN_EPS = 1e-5
NORM_EPS = 1e-6
NEG = -1e30

LANES = 128
SUBLANES = 8

COL_QA = 0
COL_KA = 1536
COL_VA = 3072
COL_ZA = 4608
COL_QKVB = 5120
COL_ZB = 8192
COL_GA = 9216
COL_GB = 10240
COL_AB = 11264
PROJ_COLS = 11520
QKVB_COLS = 3 * B_WIDTH
PROJ_TN = 2304

VMEM_LIMIT = 48 * 1024 * 1024


def _sigmoid(x):
    return 1.0 / (1.0 + jnp.exp(-x))


def _silu(x):
    return x * _sigmoid(x)


def _softplus(x):
    return jnp.maximum(x, 0.0) + jnp.log1p(jnp.exp(-jnp.abs(x)))


def _split(x):
    hi = x.astype(BF16)
    return hi, (x - hi.astype(F32)).astype(BF16)


def _dot_split(a, b):
    lhs = jnp.concatenate([a[0], a[1], a[0]], axis=1)
    rhs = jnp.concatenate([b[0], b[0], b[1]], axis=0)
    return jnp.dot(lhs, rhs, preferred_element_type=F32)


def _cond_kernel(c_ref, w_ref, b_ref, o_ref):
    s = _silu(c_ref[...]).astype(BF16)
    o_ref[...] = jnp.dot(s, w_ref[...].astype(BF16), preferred_element_type=F32) + b_ref[...]


def _cond(c, w, b):
    n, d = c.shape
    width = w.shape[1]
    tn = 512
    return pl.pallas_call(
        _cond_kernel,
        grid=(width // tn,),
        in_specs=[pl.BlockSpec((n, d), lambda j: (0, 0)),
                  pl.BlockSpec((d, tn), lambda j: (0, j)),
                  pl.BlockSpec((1, tn), lambda j: (0, j))],
        out_specs=pl.BlockSpec((n, tn), lambda j: (0, j)),
        out_shape=jax.ShapeDtypeStruct((n, width), F32),
        name="cond",
    )(c, w, b.reshape(1, width))


def _proj_kernel(x_ref, sc_ref, sh_ref, w_ref, o_ref, h_ref):
    @pl.when(pl.program_id(1) == 0)
    def _():
        h_ref[...] = (x_ref[...] * (1.0 + sc_ref[...]) + sh_ref[...]).astype(BF16)

    o_ref[...] = jnp.dot(h_ref[...], w_ref[...], preferred_element_type=F32)


def _proj(x, scale, shift, w, tm):
    m, d = x.shape
    n = w.shape[1]
    groups, r, _ = scale.shape
    blocks_per_group = (m // tm) // groups
    mod_spec = pl.BlockSpec((None, r, d), lambda i, j: (i // blocks_per_group, 0, 0))
    return pl.pallas_call(
        _proj_kernel,
        grid=(m // tm, n // PROJ_TN),
        in_specs=[pl.BlockSpec((tm, d), lambda i, j: (i, 0)),
                  mod_spec, mod_spec,
                  pl.BlockSpec((d, PROJ_TN), lambda i, j: (0, j))],
        out_specs=pl.BlockSpec((tm, PROJ_TN), lambda i, j: (i, j)),
        out_shape=jax.ShapeDtypeStruct((m, n), F32),
        scratch_shapes=[pltpu.VMEM((tm, d), BF16)],
        compiler_params=pltpu.CompilerParams(
            dimension_semantics=("arbitrary", "arbitrary"), vmem_limit_bytes=VMEM_LIMIT),
        name="proj",
    )(x, scale, shift, w)


ATT_TB = 2048


def _attn_prompt_kernel(q_ref, k_ref, kp_ref, v_ref, vp_ref, tb_ref, o_ref, t0_ref, t1_ref, t2_ref,
                        kc_ref, vc_ref, acc_ref, m_ref, l_ref):
    i = pl.program_id(1)
    gi = pl.program_id(3)
    tb = ATT_TB
    tail_refs = (t0_ref, t1_ref, t2_ref)
    kc_ref[0:tb, :] = kp_ref[...]
    kc_ref[tb:2 * tb, :] = k_ref[...]
    vc_ref[0:tb, :] = vp_ref[...]
    vc_ref[tb:2 * tb, :] = v_ref[...]

    lane = lax.broadcasted_iota(jnp.int32, (A_KEYS, LANES), 1)
    lo = lane < A_HEAD_DIM

    def run_group(d, first, last):
        tq = A_KEYS * d
        shift = int(math.log2(d))

        def body(u, carry):
            sb = u >> shift
            r = u & (d - 1)
            start = sb * tq + r
            rows = pl.ds(start, A_KEYS, stride=d)
            q = q_ref[rows, :]
            krows = pl.ds(tb - tq + start, 2 * A_KEYS, stride=d)
            kk = kc_ref[krows, :].astype(BF16)
            vv = vc_ref[krows, :].astype(BF16)
            variant = jnp.logical_and(i == 0, sb == 0).astype(jnp.int32)
            res = []
            for hh in range(2):
                sel = lo if hh == 0 else jnp.logical_not(lo)
                qh = jnp.where(sel, q * (A_HEAD_DIM ** -0.5), 0.0).astype(BF16)
                s = lax.dot_general(qh, kk, (((1,), (1,)), ((), ())), preferred_element_type=F32)
                s = s + tb_ref[variant, hh]
                mx = jnp.max(s, axis=-1, keepdims=True)
                p = jnp.exp(s - mx)
                ls = jnp.sum(p, axis=-1, keepdims=True)
                o = jnp.dot(p.astype(BF16), vv, preferred_element_type=F32)
                res.append((o, mx, ls))
            o = jnp.where(lo, res[0][0], res[1][0])
            mx = jnp.where(lo, res[0][1], res[1][1])
            ls = jnp.where(lo, res[0][2], res[1][2])
            if first:
                acc_ref[rows, :] = o
                m_ref[rows, :] = mx
                l_ref[rows, :] = ls
            else:
                mo = m_ref[rows, :]
                mn = jnp.maximum(mo, mx)
                a = jnp.exp(mo - mn)
                b = jnp.exp(mx - mn)
                acc = a * acc_ref[rows, :] + b * o
                ll = a * l_ref[rows, :] + b * ls
                if last:
                    o_ref[rows, :] = acc / ll
                else:
                    acc_ref[rows, :] = acc
                    m_ref[rows, :] = mn
                    l_ref[rows, :] = ll
            return carry

        lax.fori_loop(0, tb // A_KEYS, body, 0, unroll=8)

    def emit_tail(t_ref):
        window = t_ref.shape[2]
        for kv, src in enumerate((k_ref, v_ref)):
            for c0 in range(0, window, LANES):
                t_ref[kv, :, c0:c0 + LANES] = src[tb - window + c0:tb - window + c0 + LANES, :].T

    n_groups = len(A_GROUPS)
    for step in range(n_groups):
        g = n_groups - 1 - step

        @pl.when(gi == step)
        def _(g=g, step=step):
            run_group(A_GROUPS[g][1], step == 0, step == n_groups - 1)

        @pl.when(jnp.logical_and(gi == step, i == pl.num_programs(1) - 1))
        def _(g=g):
            emit_tail(tail_refs[g])


def _attn_prompt(p3, tables):
    bsz, seq, _ = p3.shape
    nblk = seq // ATT_TB
    n_groups = len(A_GROUPS)
    hp_per_group = A_GROUP_WIDTH // LANES
    windows = [min(w, ATT_TB) for w, _ in A_GROUPS]
    assert all(min(w, seq) == wt for (w, _), wt in zip(A_GROUPS, windows))

    def tail_idx(b, i, hp, gi):
        return (b, 0, jnp.where(i == nblk - 1, hp, 0), 0)

    def col(base):
        return lambda b, i, hp, gi: (b, i, base // LANES + (n_groups - 1 - gi) * hp_per_group + hp)

    def col_prev(base):
        return lambda b, i, hp, gi: (b, jnp.maximum(i - 1, 0),
                                     base // LANES + (n_groups - 1 - gi) * hp_per_group + hp)

    blk = (None, ATT_TB, LANES)
    return pl.pallas_call(
        _attn_prompt_kernel,
        grid=(bsz, nblk, hp_per_group, n_groups),
        in_specs=[pl.BlockSpec(blk, col(COL_QA)),
                  pl.BlockSpec(blk, col(COL_KA)),
                  pl.BlockSpec(blk, col_prev(COL_KA)),
                  pl.BlockSpec(blk, col(COL_VA)),
                  pl.BlockSpec(blk, col_prev(COL_VA)),
                  pl.BlockSpec((None, 2, 2, A_KEYS, 2 * A_KEYS),
                               lambda b, i, hp, gi: (n_groups - 1 - gi, 0, hp, 0, 0))],
        out_specs=[pl.BlockSpec(blk, lambda b, i, hp, gi: (b, i, hp))]
        + [pl.BlockSpec((None, 2, LANES, w), tail_idx) for w in windows],
        out_shape=[jax.ShapeDtypeStruct((bsz, seq, A_GROUP_WIDTH), F32)]
        + [jax.ShapeDtypeStruct((bsz, 2, A_GROUP_WIDTH, w), F32) for w in windows],
        scratch_shapes=[pltpu.VMEM((2 * ATT_TB, LANES), F32),
                        pltpu.VMEM((2 * ATT_TB, LANES), F32),
                        pltpu.VMEM((ATT_TB, LANES), F32),
                        pltpu.VMEM((ATT_TB, LANES), F32),
                        pltpu.VMEM((ATT_TB, LANES), F32)],
        compiler_params=pltpu.CompilerParams(
            dimension_semantics=("arbitrary",) * 4, vmem_limit_bytes=VMEM_LIMIT),
        name="attn_prompt",
    )(p3, p3, p3, p3, p3, tables)


ROLL_ELEMS = 64 * SUBLANES * LANES


def _attn_sample_kernel(qkv_ref, c0_ref, c1_ref, c2_ref, tc0_ref, tc1_ref, tc2_ref, tn_ref,
                        o_ref, n0_ref, n1_ref, n2_ref, nt_ref, *, n_new):
    caches = (c0_ref, c1_ref, c2_ref)
    tables = (tc0_ref, tc1_ref, tc2_ref)
    outs = (n0_ref, n1_ref, n2_ref)
    gw = A_GROUP_WIDTH
    scale = A_HEAD_DIM ** -0.5

    head_of_lane = lax.broadcasted_iota(jnp.int32, (A_HEADS, gw), 1) >> int(math.log2(A_HEAD_DIM))
    head_of_row = lax.broadcasted_iota(jnp.int32, (A_HEADS, gw), 0)
    hmask = head_of_lane == head_of_row

    parts = []
    for g, (window, d) in enumerate(A_GROUPS):
        c_ref, n_ref = caches[g], outs[g]
        q_new = qkv_ref[:, COL_QA + g * gw:COL_QA + (g + 1) * gw]
        k_new = qkv_ref[:, COL_KA + g * gw:COL_KA + (g + 1) * gw]
        v_new = qkv_ref[:, COL_VA + g * gw:COL_VA + (g + 1) * gw]

        qbd = jnp.concatenate([jnp.where(hmask, q_new[i:i + 1, :], 0.0) for i in range(n_new)], axis=0)
        qb = qbd.astype(BF16)
        s_c = jnp.dot(qb, c_ref[0:gw, :].astype(BF16), preferred_element_type=F32) * scale + tables[g][...]
        qf = qb.astype(F32)
        kf = k_new.astype(BF16).astype(F32)
        vf = v_new.astype(BF16).astype(F32)
        s_n = [jnp.sum(qf * kf[j:j + 1, :], axis=-1, keepdims=True) * scale + tn_ref[g, :, j:j + 1]
               for j in range(n_new)]
        mx = functools.reduce(jnp.maximum, s_n, jnp.max(s_c, axis=-1, keepdims=True))
        p_c = jnp.exp(s_c - mx)
        p_n = [jnp.exp(s - mx) for s in s_n]
        ls = jnp.sum(p_c, axis=-1, keepdims=True) + sum(p_n)
        o = lax.dot_general(p_c.astype(BF16), c_ref[gw:2 * gw, :].astype(BF16), (((1,), (1,)), ((), ())),
                            preferred_element_type=F32)
        for j in range(n_new):
            o = o + p_n[j].astype(BF16).astype(F32) * vf[j:j + 1, :]
        parts.append((o, mx, ls))

        for cb in range(2 * gw // LANES):
            src = k_new if cb < gw // LANES else v_new
            c0 = (cb % (gw // LANES)) * LANES
            blk = jnp.concatenate([src[:, c0:c0 + LANES], jnp.zeros((LANES - src.shape[0], LANES), F32)], axis=0)
            nt_ref[cb * LANES:(cb + 1) * LANES, :] = pltpu.roll(blk.T, LANES - n_new, axis=1)

        nrows = min(2 * gw, ROLL_ELEMS // window)
        is_new = lax.broadcasted_iota(jnp.int32, (nrows, LANES), 1) >= LANES - n_new
        for r0 in range(0, 2 * gw, nrows):
            rows = slice(r0, r0 + nrows)
            y = pltpu.roll(c_ref[rows, :], window - n_new, axis=1)
            if window > LANES:
                n_ref[rows, 0:window - LANES] = y[:, 0:window - LANES]
            n_ref[rows, window - LANES:window] = jnp.where(is_new, nt_ref[rows, :], y[:, window - LANES:window])

    mt = functools.reduce(jnp.maximum, [p[1] for p in parts])
    num = sum(jnp.exp(p[1] - mt) * p[0] for p in parts)
    den = sum(jnp.exp(p[1] - mt) * p[2] for p in parts)
    comb = num / den
    for i in range(n_new):
        sel = jnp.where(hmask, comb[i * A_HEADS:(i + 1) * A_HEADS, :], 0.0)
        o_ref[i:i + 1, :] = jnp.sum(sel, axis=0, keepdims=True)


def _attn_sample(qkv, n_new, c0, c1, c2, tabs_c, tab_n):
    dbsz, rows, qkv_w = qkv.shape
    nq = n_new * A_HEADS

    def cspec(w):
        return pl.BlockSpec((None, 2 * A_GROUP_WIDTH, w), lambda b: (b, 0, 0))

    ws = [g[0] for g in A_GROUPS]
    return pl.pallas_call(
        functools.partial(_attn_sample_kernel, n_new=n_new),
        grid=(dbsz,),
        in_specs=[pl.BlockSpec((None, rows, qkv_w), lambda b: (b, 0, 0)),
                  cspec(ws[0]), cspec(ws[1]), cspec(ws[2])]
        + [pl.BlockSpec((nq, w), lambda b: (0, 0)) for w in ws]
        + [pl.BlockSpec((len(A_GROUPS), nq, LANES), lambda b: (0, 0, 0))],
        out_specs=[pl.BlockSpec((None, n_new, A_GROUP_WIDTH), lambda b: (b, 0, 0)),
                   cspec(ws[0]), cspec(ws[1]), cspec(ws[2])],
        out_shape=[jax.ShapeDtypeStruct((dbsz, n_new, A_GROUP_WIDTH), F32)]
        + [jax.ShapeDtypeStruct(c.shape, c.dtype) for c in (c0, c1, c2)],
        scratch_shapes=[pltpu.VMEM((2 * A_GROUP_WIDTH, LANES), F32)],
        compiler_params=pltpu.CompilerParams(
            dimension_semantics=("arbitrary",), vmem_limit_bytes=VMEM_LIMIT),
        name="attn_sample",
    )(qkv, c0, c1, c2, *tabs_c, tab_n)


def _delta_kernel(xq_ref, xk_ref, xv_ref, ab_ref, hist_ref, s0_ref, cw_ref, gp_ref, o_ref, st_ref, xs_ref,
                  *, cs, n_valid):
    c = pl.program_id(1)
    n_sub = xq_ref.shape[0] // cs
    hd = B_HEAD_DIM
    heads = range(B_HEADS)
    nt_dims = (((1,), (1,)), ((), ()))

    @pl.when(c == 0)
    def _():
        st_ref[...] = s0_ref[...]
        xs_ref[0:SUBLANES, :] = hist_ref[...]

    for part, ref in enumerate((xq_ref, xk_ref, xv_ref)):
        xs_ref[SUBLANES:SUBLANES + n_sub * cs, part * B_WIDTH:(part + 1) * B_WIDTH] = ref[...]

    def activated(row0, part, h):
        cols = slice(part * B_WIDTH + h * hd, part * B_WIDTH + (h + 1) * hd)
        first = row0 + SUBLANES - (CONV_W - 1)
        y = xs_ref[pl.ds(first, cs), cols] * cw_ref[0:1, cols]
        for t in range(1, CONV_W):
            y = y + xs_ref[pl.ds(first + t, cs), cols] * cw_ref[t:t + 1, cols]
        return _silu(y)

    def l2norm(t):
        return t * lax.rsqrt(jnp.sum(t * t, axis=-1, keepdims=True) + NORM_EPS)

    ri = lax.broadcasted_iota(jnp.int32, (cs, cs), 0)
    ci = lax.broadcasted_iota(jnp.int32, (cs, cs), 1)
    causal = ri >= ci
    strict = ri > ci
    eye = (ri == ci).astype(F32)

    pre = []
    for sub in range(n_sub):
        row0 = sub * cs
        ab = ab_ref[row0:row0 + cs, :]
        g_all = -jnp.exp(gp_ref[0:1, :]) * _softplus(ab + gp_ref[1:2, :])
        beta_all = _sigmoid(ab)
        if n_valid < cs:
            live = lax.broadcasted_iota(jnp.int32, (cs, LANES), 0) < n_valid
            g_all = jnp.where(live, g_all, 0.0)
            beta_all = jnp.where(live, beta_all, 0.0)
        cum = jnp.dot(causal.astype(F32), g_all, precision=HIGHEST, preferred_element_type=F32)
        cum_sq = jnp.concatenate([cum, jnp.zeros((LANES - cs, LANES), F32)], axis=0) if cs < LANES else cum
        cum_t = cum_sq.T
        ecum = jnp.exp(cum)
        cum_last = cum[cs - 1:cs, :]
        tail = jnp.exp(cum_last - cum)
        elast = jnp.exp(cum_last)
        for h in heads:
            qh = l2norm(activated(row0, 0, h)) * (hd ** -0.5)
            kh = l2norm(activated(row0, 1, h))
            vh = activated(row0, 2, h)
            beta = beta_all[:, SUBLANES + h:SUBLANES + h + 1]
            ec = ecum[:, h:h + 1]
            decay = jnp.exp(jnp.where(causal, cum[:, h:h + 1] - cum_t[h:h + 1, 0:cs], NEG))
            kb = kh.astype(BF16)
            kk = lax.dot_general(kb, kb, nt_dims, preferred_element_type=F32)
            pre.append(dict(
                sub=sub, h=h, kb=kb, decay=decay, qb=qh.astype(BF16), qe=(qh * ec).astype(BF16),
                kt=(kh * tail[:, h:h + 1]).astype(BF16), elast=elast[:, h:h + 1],
                low=-jnp.where(strict, beta * kk * decay, 0.0),
                rhs=jnp.concatenate([beta * vh, beta * kh * ec], axis=1)))

    tinvs = [eye + e["low"] for e in pre]
    pbs = [e["low"].astype(BF16) for e in pre]
    for _ in range(int(math.log2(cs)) - 1):
        pbs = [jnp.dot(p, p, preferred_element_type=F32).astype(BF16) for p in pbs]
        tinvs = [t + jnp.dot(t.astype(BF16), p, preferred_element_type=F32) for t, p in zip(tinvs, pbs)]
    tbs = [t.astype(BF16) for t in tinvs]
    sol0 = [jnp.dot(t, e["rhs"].astype(BF16), preferred_element_type=F32) for t, e in zip(tbs, pre)]
    resid = [e["rhs"] - s0 + _dot_split(_split(e["low"]), _split(s0)) for e, s0 in zip(pre, sol0)]
    sols = [s0 + jnp.dot(t, r.astype(BF16), preferred_element_type=F32) for s0, t, r in zip(sol0, tbs, resid)]

    for e, sol in zip(pre, sols):
        h, row0 = e["h"], e["sub"] * cs
        u = sol[:, 0:hd]
        w = sol[:, hd:2 * hd]
        state = st_ref[h]
        sb = state.astype(BF16)
        v_new = u - jnp.dot(w.astype(BF16), sb, preferred_element_type=F32)
        vb = v_new.astype(BF16)
        qk = lax.dot_general(e["qb"], e["kb"], nt_dims, preferred_element_type=F32) * e["decay"]
        out = jnp.dot(e["qe"], sb, preferred_element_type=F32)
        out = out + jnp.dot(qk.astype(BF16), vb, preferred_element_type=F32)
        o_ref[row0:row0 + cs, h * hd:(h + 1) * hd] = out
        st_ref[h] = e["elast"] * state + lax.dot_general(
            e["kt"], vb, (((0,), (0,)), ((), ())), preferred_element_type=F32)

    xs_ref[0:SUBLANES, :] = xs_ref[n_sub * cs:n_sub * cs + SUBLANES, :]


def _delta(p3, col_qkv, col_ab, hist, state0, conv_w, gate_par, cs, n_sub, n_valid):
    nb, t, _ = p3.shape
    rows = cs * n_sub
    qkv_w = 3 * B_WIDTH

    def xspec(part):
        return pl.BlockSpec((None, rows, B_WIDTH), lambda b, c: (b, c, col_qkv // B_WIDTH + part))

    return pl.pallas_call(
        functools.partial(_delta_kernel, cs=cs, n_valid=n_valid),
        grid=(nb, t // rows),
        in_specs=[xspec(0), xspec(1), xspec(2),
                  pl.BlockSpec((None, rows, LANES), lambda b, c: (b, c, col_ab // LANES)),
                  pl.BlockSpec((None, SUBLANES, qkv_w), lambda b, c: (b, 0, 0)),
                  pl.BlockSpec((None, B_HEADS, B_HEAD_DIM, B_HEAD_DIM), lambda b, c: (b, 0, 0, 0)),
                  pl.BlockSpec((CONV_W, qkv_w), lambda b, c: (0, 0)),
                  pl.BlockSpec((SUBLANES, LANES), lambda b, c: (0, 0))],
        out_specs=[pl.BlockSpec((None, rows, B_WIDTH), lambda b, c: (b, c, 0)),
                   pl.BlockSpec((None, B_HEADS, B_HEAD_DIM, B_HEAD_DIM), lambda b, c: (b, 0, 0, 0))],
        out_shape=[jax.ShapeDtypeStruct((nb, t, B_WIDTH), F32),
                   jax.ShapeDtypeStruct((nb, B_HEADS, B_HEAD_DIM, B_HEAD_DIM), F32)],
        scratch_shapes=[pltpu.VMEM((SUBLANES + rows, qkv_w), F32)],
        compiler_params=pltpu.CompilerParams(
            dimension_semantics=("arbitrary", "arbitrary"), vmem_limit_bytes=VMEM_LIMIT),
        name="delta",
    )(p3, p3, p3, p3, hist, state0, conv_w, gate_par)


def _back_kernel(x_ref, gate_ref, oa_ref, za_ref, ob_ref, zb_ref, ga_ref, gb_ref,
                 wa_ref, wb_ref, wo_ref, bn_ref, lg_ref, lb_ref, y_ref):
    ya = (oa_ref[...] * _silu(za_ref[...])).astype(BF16)
    parts = []
    for h in range(B_HEADS):
        cols = slice(h * B_HEAD_DIM, (h + 1) * B_HEAD_DIM)
        t = ob_ref[:, cols]
        t = t * lax.rsqrt(jnp.mean(t * t, axis=-1, keepdims=True) + NORM_EPS) * bn_ref[...]
        parts.append((t * _silu(zb_ref[:, cols])).astype(BF16))
    yb = jnp.concatenate(parts, axis=1)
    ma = jnp.dot(ya, wa_ref[...], preferred_element_type=F32)
    mb = jnp.dot(yb, wb_ref[...], preferred_element_type=F32)
    merged = _sigmoid(ga_ref[...]) * ma + _sigmoid(gb_ref[...]) * mb
    r = ALPHA * x_ref[...] + gate_ref[...] * jnp.dot(merged.astype(BF16), wo_ref[...], preferred_element_type=F32)
    mu = jnp.mean(r, axis=-1, keepdims=True)
    rc = r - mu
    var = jnp.mean(rc * rc, axis=-1, keepdims=True)
    y_ref[...] = rc * lax.rsqrt(var + LN_EPS) * lg_ref[...] + lb_ref[...]


def _back(x, gate, oa, p, ob, wa, wb, wo, bn, lg, lb, tm):
    m, d = x.shape
    groups, r, _ = gate.shape
    blocks_per_group = (m // tm) // groups

    def const(shape):
        return pl.BlockSpec(shape, lambda i: (0,) * len(shape))

    return pl.pallas_call(
        _back_kernel,
        grid=(m // tm,),
        in_specs=[pl.BlockSpec((tm, d), lambda i: (i, 0)),
                  pl.BlockSpec((None, r, d), lambda i: (i // blocks_per_group, 0, 0)),
                  pl.BlockSpec((tm, A_GROUP_WIDTH), lambda i: (i, 0)),
                  pl.BlockSpec((tm, A_GROUP_WIDTH), lambda i: (i, COL_ZA // A_GROUP_WIDTH)),
                  pl.BlockSpec((tm, B_WIDTH), lambda i: (i, 0)),
                  pl.BlockSpec((tm, B_WIDTH), lambda i: (i, COL_ZB // B_WIDTH)),
                  pl.BlockSpec((tm, d), lambda i: (i, COL_GA // d)),
                  pl.BlockSpec((tm, d), lambda i: (i, COL_GB // d)),
                  const(wa.shape), const(wb.shape), const(wo.shape),
                  const((1, B_HEAD_DIM)), const((1, d)), const((1, d))],
        out_specs=pl.BlockSpec((tm, d), lambda i: (i, 0)),
        out_shape=jax.ShapeDtypeStruct((m, d), F32),
        compiler_params=pltpu.CompilerParams(
            dimension_semantics=("arbitrary",), vmem_limit_bytes=VMEM_LIMIT),
        name="back",
    )(x, gate, oa, p, ob, p, p, p, wa, wb, wo, bn, lg, lb)


def _t5_causal_buckets(dist):
    max_exact = N_BUCKETS // 2
    dist = np.asarray(dist, dtype=np.int64)
    ratio = np.maximum(dist, max_exact) / max_exact
    large = max_exact + (np.log(ratio) / math.log(BUCKET_MAX_DIST / max_exact) * (N_BUCKETS - max_exact)).astype(np.int64)
    return np.where(dist < max_exact, dist, np.minimum(large, N_BUCKETS - 1)).astype(np.int32)


def _group_bias(rel_bias, gi):
    window, dil = A_GROUPS[gi]
    buckets = _t5_causal_buckets(dil * np.arange(window // dil + 1))
    onehot = (buckets[:, None] == np.arange(N_BUCKETS)[None, :]).astype(np.float32)
    sel = jnp.dot(onehot, rel_bias[:, gi * A_HEADS:(gi + 1) * A_HEADS], precision=HIGHEST)
    return sel.T


def _neg(*shape):
    return jnp.full(shape, NEG, F32)


def _prompt_tables(rel_bias):
    n = A_KEYS
    tabs = []
    for gi in range(len(A_GROUPS)):
        b = _group_bias(rel_bias, gi)
        c = jnp.concatenate([_neg(A_HEADS, n), b[:, ::-1], _neg(A_HEADS, n - 1)], axis=1)
        flat = jnp.tile(c[:, 1:], (1, n + 1))[:, 0:n * 3 * n]
        hank = flat.reshape(A_HEADS, n, 3 * n)[:, :, 0:2 * n]
        tab = hank[:, ::-1, :]
        tabs.append(jnp.stack([tab, jnp.concatenate([_neg(A_HEADS, n, n), tab[:, :, n:]], axis=2)]))
    return jnp.stack(tabs)


def _sample_tables(rel_bias, n_new):
    tabs_c, tabs_n = [], []
    for gi, (window, dil) in enumerate(A_GROUPS):
        b = _group_bias(rel_bias, gi)
        rev = b[:, :0:-1]
        if dil > 1:
            rev = jnp.concatenate([rev[:, :, None], _neg(A_HEADS, A_KEYS, dil - 1)], axis=2).reshape(A_HEADS, window)
        rows_c, rows_n = [], []
        for i in range(n_new):
            rows_c.append(jnp.concatenate([_neg(A_HEADS, i), rev[:, 0:window - i]], axis=1) if i else rev)
            cols = [b[:, (i - j) // dil][:, None] if (j <= i and (i - j) % dil == 0) else _neg(A_HEADS, 1)
                    for j in range(n_new)]
            rows_n.append(jnp.concatenate(cols + [_neg(A_HEADS, LANES - n_new)], axis=1))
        tabs_c.append(jnp.concatenate(rows_c, axis=0))
        tabs_n.append(jnp.concatenate(rows_n, axis=0))
    return tabs_c, jnp.stack(tabs_n)


def _regroup_w_in(w_in):
    main = w_in[:, 0:9216]
    ab = w_in[:, 9216:9232]
    gates = w_in[:, 9232:11280]
    pad = jnp.zeros((w_in.shape[0], PROJ_COLS - (COL_AB + 2 * B_HEADS)), w_in.dtype)
    return jnp.concatenate([main, gates, ab, pad], axis=1).astype(BF16)


def _cache_from_time_minor(t):
    bsz, _, _, w = t.shape
    return jnp.transpose(t.reshape(bsz, 2, A_HEADS, A_HEAD_DIM, w), (0, 4, 1, 2, 3))[None]


def kernel(x_prompt, x_sample, c_prompt, c_sample, cache_kv_w128, cache_kv_w512, cache_kv_w2048, state_conv, state_delta, w_cond, b_cond, w_in, rel_bias, conv_w, a_log, dt_bias, b_norm_w, w_branch_a, w_branch_b, w_out, ln_g, ln_b):
    bsz, seq, d = x_prompt.shape
    dbsz, dseq, _ = x_sample.shape
    layer = 0

    c_all = jnp.concatenate([c_prompt, c_sample], axis=0)
    c_all = jnp.pad(c_all, ((0, -(bsz + dbsz) % 16), (0, 0)))
    cond = _cond(c_all, w_cond[layer], b_cond[layer])
    shift, scale, gate = cond[:, 0:d], cond[:, d:2 * d], cond[:, 2 * d:3 * d]

    w_in_r = _regroup_w_in(w_in[layer])
    wa = w_branch_a[layer].astype(BF16)
    wb = w_branch_b[layer].astype(BF16)
    wo = w_out[layer].astype(BF16)
    bn = b_norm_w[layer].reshape(1, B_HEAD_DIM)
    lg = ln_g[layer].reshape(1, d)
    lb = ln_b[layer].reshape(1, d)
    gate_par = jnp.zeros((SUBLANES, LANES), F32)
    gate_par = gate_par.at[0, 0:B_HEADS].set(a_log[layer]).at[1, 0:B_HEADS].set(dt_bias[layer])
    cw = conv_w[layer]

    xp = x_prompt.reshape(bsz * seq, d)
    pp = _proj(xp, scale[0:bsz, None, :], shift[0:bsz, None, :], w_in_r, tm=1024)
    pp3 = pp.reshape(bsz, seq, PROJ_COLS)
    oa_p, *tails_p = _attn_prompt(pp3, _prompt_tables(rel_bias))
    hist0 = jnp.zeros((bsz, SUBLANES, QKVB_COLS), F32)
    st0 = jnp.zeros((bsz, B_HEADS, B_HEAD_DIM, B_HEAD_DIM), F32)
    ob_p, st_p = _delta(pp3, COL_QKVB, COL_AB, hist0, st0, cw, gate_par, B_CHUNK, 2, B_CHUNK)
    y_p = _back(xp, gate[0:bsz, None, :], oa_p.reshape(bsz * seq, A_GROUP_WIDTH), pp,
                ob_p.reshape(bsz * seq, B_WIDTH), wa, wb, wo, bn, lg, lb, tm=512)
    y_prompt = y_p.reshape(bsz, seq, d)
    kv_p = [_cache_from_time_minor(t) for t in tails_p]
    conv_p = pp3[:, seq - (CONV_W - 1):, COL_QKVB:COL_QKVB + QKVB_COLS][None]
    delta_p = st_p[None]

    ms = dbsz * dseq
    xs = x_sample.reshape(ms, d)
    rep = lambda t: jnp.repeat(t[bsz:bsz + dbsz], dseq, axis=0)[None]
    ps = _proj(xs, rep(scale), rep(shift), w_in_r, tm=ms)
    ps3 = ps.reshape(dbsz, dseq, PROJ_COLS)
    cache_in = (cache_kv_w128, cache_kv_w512, cache_kv_w2048)
    caches = [jnp.transpose(c[layer], (0, 2, 3, 4, 1)).reshape(dbsz, 2 * A_GROUP_WIDTH, c.shape[2]) for c in cache_in]
    qkv_s = jnp.pad(ps3[:, :, 0:COL_ZA], ((0, 0), (0, 16 - dseq), (0, 0)))
    tabs_c, tab_n = _sample_tables(rel_bias, dseq)
    oa_s, n0, n1, n2 = _attn_sample(qkv_s, dseq, caches[0], caches[1], caches[2], tabs_c, tab_n)
    kv_s = [_cache_from_time_minor(n.reshape(dbsz, 2, A_GROUP_WIDTH, n.shape[2])) for n in (n0, n1, n2)]
    cs_s = SUBLANES
    pad_rows = ((0, 0), (0, cs_s - dseq), (0, 0))
    hist_s = jnp.pad(state_conv[layer], ((0, 0), (SUBLANES - (CONV_W - 1), 0), (0, 0)))
    pd = jnp.concatenate([ps3[:, :, COL_QKVB:COL_QKVB + QKVB_COLS], ps3[:, :, COL_AB:COL_AB + LANES]], axis=-1)
    ob_s, st_s = _delta(jnp.pad(pd, pad_rows), 0, QKVB_COLS, hist_s, state_delta[layer], cw, gate_par,
                        cs_s, 1, dseq)
    y_s = _back(xs, rep(gate), oa_s.reshape(ms, A_GROUP_WIDTH), ps,
                ob_s[:, 0:dseq].reshape(ms, B_WIDTH), wa, wb, wo, bn, lg, lb, tm=ms)
    y_sample = y_s.reshape(dbsz, dseq, d)
    conv_s = ps3[:, dseq - (CONV_W - 1):, COL_QKVB:COL_QKVB + QKVB_COLS][None]
    delta_s = st_s[None]

    return (y_prompt, y_sample, kv_p[0], kv_p[1], kv_p[2], conv_p, delta_p,
            kv_s[0], kv_s[1], kv_s[2], conv_s, delta_s)
```

```python
import functools
import math

import numpy as np
import jax
import jax.numpy as jnp
from jax import lax
from jax.experimental import pallas as pl
from jax.experimental.pallas import tpu as pltpu

F32 = jnp.float32
BF16 = jnp.bfloat16
HIGHEST = lax.Precision.HIGHEST

D_MODEL = 1024
A_GROUPS = ((128, 1), (512, 4), (2048, 16))
A_HEADS = 8
A_HEAD_DIM = 64
A_GROUP_WIDTH = A_HEADS * A_HEAD_DIM
A_KEYS = 128
N_BUCKETS = 32
BUCKET_MAX_DIST = 2048
B_HEADS = 8
B_HEAD_DIM = 128
B_WIDTH = B_HEADS * B_HEAD_DIM
CONV_W = 4
B_CHUNK = 64
DEPTH = 1
ALPHA = (2 * DEPTH) ** 0.25
LN_EPS = 1e-5
NORM_EPS = 1e-6
NEG = -1e30

LANES = 128
SUBLANES = 8

COL_QA = 0
COL_KA = 1536
COL_VA = 3072
COL_ZA = 4608
COL_QKVB = 5120
COL_ZB = 8192
COL_GA = 9216
COL_GB = 10240
COL_AB = 11264
PROJ_COLS = 11520
QKVB_COLS = 3 * B_WIDTH
PROJ_TN = 2304

VMEM_LIMIT = 48 * 1024 * 1024


def _sigmoid(x):
    return 1.0 / (1.0 + jnp.exp(-x))


def _silu(x):
    return x * _sigmoid(x)


def _softplus(x):
    return jnp.maximum(x, 0.0) + jnp.log1p(jnp.exp(-jnp.abs(x)))


def _split(x):
    hi = x.astype(BF16)
    return hi, (x - hi.astype(F32)).astype(BF16)


def _dot_split(a, b):
    lhs = jnp.concatenate([a[0], a[1], a[0]], axis=1)
    rhs = jnp.concatenate([b[0], b[0], b[1]], axis=0)
    return jnp.dot(lhs, rhs, preferred_element_type=F32)


def _cond_kernel(c_ref, w_ref, b_ref, o_ref):
    s = _silu(c_ref[...]).astype(BF16)
    o_ref[...] = jnp.dot(s, w_ref[...].astype(BF16), preferred_element_type=F32) + b_ref[...]


def _cond(c, w, b):
    n, d = c.shape
    width = w.shape[1]
    tn = 512
    return pl.pallas_call(
        _cond_kernel,
        grid=(width // tn,),
        in_specs=[pl.BlockSpec((n, d), lambda j: (0, 0)),
                  pl.BlockSpec((d, tn), lambda j: (0, j)),
                  pl.BlockSpec((1, tn), lambda j: (0, j))],
        out_specs=pl.BlockSpec((n, tn), lambda j: (0, j)),
        out_shape=jax.ShapeDtypeStruct((n, width), F32),
        name="cond",
    )(c, w, b.reshape(1, width))


def _proj_kernel(x_ref, sc_ref, sh_ref, w_ref, o_ref, h_ref):
    @pl.when(pl.program_id(1) == 0)
    def _():
        h_ref[...] = (x_ref[...] * (1.0 + sc_ref[...]) + sh_ref[...]).astype(BF16)

    o_ref[...] = jnp.dot(h_ref[...], w_ref[...], preferred_element_type=F32)


def _proj(x, scale, shift, w, tm):
    m, d = x.shape
    n = w.shape[1]
    groups, r, _ = scale.shape
    blocks_per_group = (m // tm) // groups
    mod_spec = pl.BlockSpec((None, r, d), lambda i, j: (i // blocks_per_group, 0, 0))
    return pl.pallas_call(
        _proj_kernel,
        grid=(m // tm, n // PROJ_TN),
        in_specs=[pl.BlockSpec((tm, d), lambda i, j: (i, 0)),
                  mod_spec, mod_spec,
                  pl.BlockSpec((d, PROJ_TN), lambda i, j: (0, j))],
        out_specs=pl.BlockSpec((tm, PROJ_TN), lambda i, j: (i, j)),
        out_shape=jax.ShapeDtypeStruct((m, n), F32),
        scratch_shapes=[pltpu.VMEM((tm, d), BF16)],
        compiler_params=pltpu.CompilerParams(
            dimension_semantics=("arbitrary", "arbitrary"), vmem_limit_bytes=VMEM_LIMIT),
        name="proj",
    )(x, scale, shift, w)


ATT_TB = 2048


def _attn_prompt_kernel(q_ref, k_ref, kp_ref, v_ref, vp_ref, tb_ref, o_ref, t0_ref, t1_ref, t2_ref,
                        acc_ref, m_ref, l_ref):
    i = pl.program_id(1)
    gi = pl.program_id(3)
    tb = ATT_TB
    tail_refs = (t0_ref, t1_ref, t2_ref)

    lane = lax.broadcasted_iota(jnp.int32, (A_KEYS, LANES), 1)
    lo = lane < A_HEAD_DIM

    def run_group(d, first, last):
        tq = A_KEYS * d

        def body(sb, r):
            start = sb * tq + r
            rows = pl.ds(start, A_KEYS, stride=d)
            q = q_ref[rows, :]
            if sb == 0:
                prows = pl.ds(tb - tq + r, A_KEYS, stride=d)
                kprev, vprev = kp_ref[prows, :], vp_ref[prows, :]
            else:
                prows = pl.ds(start - tq, A_KEYS, stride=d)
                kprev, vprev = k_ref[prows, :], v_ref[prows, :]
            kk = jnp.concatenate([kprev, k_ref[rows, :]], axis=0).astype(BF16)
            vv = jnp.concatenate([vprev, v_ref[rows, :]], axis=0).astype(BF16)
            variant = (i == 0).astype(jnp.int32) if sb == 0 else 0
            res = []
            for hh in range(2):
                sel = lo if hh == 0 else jnp.logical_not(lo)
                qh = jnp.where(sel, q * (A_HEAD_DIM ** -0.5), 0.0).astype(BF16)
                s = lax.dot_general(qh, kk, (((1,), (1,)), ((), ())), preferred_element_type=F32)
                s = s + tb_ref[variant, hh]
                mx = jnp.max(s, axis=-1, keepdims=True)
                p = jnp.exp(s - mx)
                ls = jnp.sum(p, axis=-1, keepdims=True)
                o = jnp.dot(p.astype(BF16), vv, preferred_element_type=F32)
                res.append((o, mx, ls))
            o = jnp.where(lo, res[0][0], res[1][0])
            mx = jnp.where(lo, res[0][1], res[1][1])
            ls = jnp.where(lo, res[0][2], res[1][2])
            if first:
                acc_ref[rows, :] = o
                m_ref[rows, :] = mx
                l_ref[rows, :] = ls
            else:
                mo = m_ref[rows, :]
                mn = jnp.maximum(mo, mx)
                a = jnp.exp(mo - mn)
                b = jnp.exp(mx - mn)
                acc = a * acc_ref[rows, :] + b * o
                ll = a * l_ref[rows, :] + b * ls
                if last:
                    o_ref[rows, :] = acc / ll
                else:
                    acc_ref[rows, :] = acc
                    m_ref[rows, :] = mn
                    l_ref[rows, :] = ll

        for u in range(tb // A_KEYS):
            body(u // d, u % d)

    def emit_tail(t_ref):
        window = t_ref.shape[2]
        for kv, src in enumerate((k_ref, v_ref)):
            for c0 in range(0, window, LANES):
                t_ref[kv, :, c0:c0 + LANES] = src[tb - window + c0:tb - window + c0 + LANES, :].T

    n_groups = len(A_GROUPS)
    for step in range(n_groups):
        g = n_groups - 1 - step

        @pl.when(gi == step)
        def _(g=g, step=step):
            run_group(A_GROUPS[g][1], step == 0, step == n_groups - 1)

        @pl.when(jnp.logical_and(gi == step, i == pl.num_programs(1) - 1))
        def _(g=g):
            emit_tail(tail_refs[g])


def _attn_prompt(p3, tables):
    bsz, seq, _ = p3.shape
    nblk = seq // ATT_TB
    n_groups = len(A_GROUPS)
    hp_per_group = A_GROUP_WIDTH // LANES
    windows = [min(w, ATT_TB) for w, _ in A_GROUPS]
    assert all(min(w, seq) == wt for (w, _), wt in zip(A_GROUPS, windows))

    def tail_idx(b, i, hp, gi):
        return (b, 0, jnp.where(i == nblk - 1, hp, 0), 0)

    def col(base):
        return lambda b, i, hp, gi: (b, i, base // LANES + (n_groups - 1 - gi) * hp_per_group + hp)

    def col_prev(base):
        return lambda b, i, hp, gi: (b, jnp.maximum(i - 1, 0),
                                     base // LANES + (n_groups - 1 - gi) * hp_per_group + hp)

    blk = (None, ATT_TB, LANES)
    return pl.pallas_call(
        _attn_prompt_kernel,
        grid=(bsz, nblk, hp_per_group, n_groups),
        in_specs=[pl.BlockSpec(blk, col(COL_QA)),
                  pl.BlockSpec(blk, col(COL_KA)),
                  pl.BlockSpec(blk, col_prev(COL_KA)),
                  pl.BlockSpec(blk, col(COL_VA)),
                  pl.BlockSpec(blk, col_prev(COL_VA)),
                  pl.BlockSpec((None, 2, 2, A_KEYS, 2 * A_KEYS),
                               lambda b, i, hp, gi: (n_groups - 1 - gi, 0, hp, 0, 0))],
        out_specs=[pl.BlockSpec(blk, lambda b, i, hp, gi: (b, i, hp))]
        + [pl.BlockSpec((None, 2, LANES, w), tail_idx) for w in windows],
        out_shape=[jax.ShapeDtypeStruct((bsz, seq, A_GROUP_WIDTH), F32)]
        + [jax.ShapeDtypeStruct((bsz, 2, A_GROUP_WIDTH, w), F32) for w in windows],
        scratch_shapes=[pltpu.VMEM((ATT_TB, LANES), F32),
                        pltpu.VMEM((ATT_TB, LANES), F32),
                        pltpu.VMEM((ATT_TB, LANES), F32)],
        compiler_params=pltpu.CompilerParams(
            dimension_semantics=("arbitrary",) * 4, vmem_limit_bytes=VMEM_LIMIT),
        name="attn_prompt",
    )(p3, p3, p3, p3, p3, tables)


ROLL_ELEMS = 64 * SUBLANES * LANES


def _attn_sample_kernel(qkv_ref, c0_ref, c1_ref, c2_ref, tc0_ref, tc1_ref, tc2_ref, tn_ref,
                        o_ref, n0_ref, n1_ref, n2_ref, nt_ref, *, n_new):
    caches = (c0_ref, c1_ref, c2_ref)
    tables = (tc0_ref, tc1_ref, tc2_ref)
    outs = (n0_ref, n1_ref, n2_ref)
    gw = A_GROUP_WIDTH
    scale = A_HEAD_DIM ** -0.5

    head_of_lane = lax.broadcasted_iota(jnp.int32, (A_HEADS, gw), 1) >> int(math.log2(A_HEAD_DIM))
    head_of_row = lax.broadcasted_iota(jnp.int32, (A_HEADS, gw), 0)
    hmask = head_of_lane == head_of_row

    parts = []
    for g, (window, d) in enumerate(A_GROUPS):
        c_ref, n_ref = caches[g], outs[g]
        q_new = qkv_ref[:, COL_QA + g * gw:COL_QA + (g + 1) * gw]
        k_new = qkv_ref[:, COL_KA + g * gw:COL_KA + (g + 1) * gw]
        v_new = qkv_ref[:, COL_VA + g * gw:COL_VA + (g + 1) * gw]

        qbd = jnp.concatenate([jnp.where(hmask, q_new[i:i + 1, :], 0.0) for i in range(n_new)], axis=0)
        qb = qbd.astype(BF16)
        s_c = jnp.dot(qb, c_ref[0:gw, :].astype(BF16), preferred_element_type=F32) * scale + tables[g][...]
        qf = qb.astype(F32)
        kf = k_new.astype(BF16).astype(F32)
        vf = v_new.astype(BF16).astype(F32)
        s_n = [jnp.sum(qf * kf[j:j + 1, :], axis=-1, keepdims=True) * scale + tn_ref[g, :, j:j + 1]
               for j in range(n_new)]
        mx = functools.reduce(jnp.maximum, s_n, jnp.max(s_c, axis=-1, keepdims=True))
        p_c = jnp.exp(s_c - mx)
        p_n = [jnp.exp(s - mx) for s in s_n]
        ls = jnp.sum(p_c, axis=-1, keepdims=True) + sum(p_n)
        o = lax.dot_general(p_c.astype(BF16), c_ref[gw:2 * gw, :].astype(BF16), (((1,), (1,)), ((), ())),
                            preferred_element_type=F32)
        for j in range(n_new):
            o = o + p_n[j].astype(BF16).astype(F32) * vf[j:j + 1, :]
        parts.append((o, mx, ls))

        for cb in range(2 * gw // LANES):
            src = k_new if cb < gw // LANES else v_new
            c0 = (cb % (gw // LANES)) * LANES
            blk = jnp.concatenate([src[:, c0:c0 + LANES], jnp.zeros((LANES - src.shape[0], LANES), F32)], axis=0)
            nt_ref[cb * LANES:(cb + 1) * LANES, :] = pltpu.roll(blk.T, LANES - n_new, axis=1)

        nrows = min(2 * gw, ROLL_ELEMS // window)
        is_new = lax.broadcasted_iota(jnp.int32, (nrows, LANES), 1) >= LANES - n_new
        for r0 in range(0, 2 * gw, nrows):
            rows = slice(r0, r0 + nrows)
            y = pltpu.roll(c_ref[rows, :], window - n_new, axis=1)
            if window > LANES:
                n_ref[rows, 0:window - LANES] = y[:, 0:window - LANES]
            n_ref[rows, window - LANES:window] = jnp.where(is_new, nt_ref[rows, :], y[:, window - LANES:window])

    mt = functools.reduce(jnp.maximum, [p[1] for p in parts])
    num = sum(jnp.exp(p[1] - mt) * p[0] for p in parts)
    den = sum(jnp.exp(p[1] - mt) * p[2] for p in parts)
    comb = num / den
    for i in range(n_new):
        sel = jnp.where(hmask, comb[i * A_HEADS:(i + 1) * A_HEADS, :], 0.0)
        o_ref[i:i + 1, :] = jnp.sum(sel, axis=0, keepdims=True)


def _attn_sample(qkv, n_new, c0, c1, c2, tabs_c, tab_n):
    dbsz, rows, qkv_w = qkv.shape
    nq = n_new * A_HEADS

    def cspec(w):
        return pl.BlockSpec((None, 2 * A_GROUP_WIDTH, w), lambda b: (b, 0, 0))

    ws = [g[0] for g in A_GROUPS]
    return pl.pallas_call(
        functools.partial(_attn_sample_kernel, n_new=n_new),
        grid=(dbsz,),
        in_specs=[pl.BlockSpec((None, rows, qkv_w), lambda b: (b, 0, 0)),
                  cspec(ws[0]), cspec(ws[1]), cspec(ws[2])]
        + [pl.BlockSpec((nq, w), lambda b: (0, 0)) for w in ws]
        + [pl.BlockSpec((len(A_GROUPS), nq, LANES), lambda b: (0, 0, 0))],
        out_specs=[pl.BlockSpec((None, n_new, A_GROUP_WIDTH), lambda b: (b, 0, 0)),
                   cspec(ws[0]), cspec(ws[1]), cspec(ws[2])],
        out_shape=[jax.ShapeDtypeStruct((dbsz, n_new, A_GROUP_WIDTH), F32)]
        + [jax.ShapeDtypeStruct(c.shape, c.dtype) for c in (c0, c1, c2)],
        scratch_shapes=[pltpu.VMEM((2 * A_GROUP_WIDTH, LANES), F32)],
        compiler_params=pltpu.CompilerParams(
            dimension_semantics=("arbitrary",), vmem_limit_bytes=VMEM_LIMIT),
        name="attn_sample",
    )(qkv, c0, c1, c2, *tabs_c, tab_n)


def _delta_kernel(xq_ref, xk_ref, xv_ref, ab_ref, hist_ref, s0_ref, cw_ref, gp_ref, o_ref, st_ref, xs_ref,
                  *, cs, n_valid):
    c = pl.program_id(1)
    n_sub = xq_ref.shape[0] // cs
    hd = B_HEAD_DIM
    heads = range(B_HEADS)
    nt_dims = (((1,), (1,)), ((), ()))

    @pl.when(c == 0)
    def _():
        st_ref[...] = s0_ref[...]
        xs_ref[0:SUBLANES, :] = hist_ref[...]

    for part, ref in enumerate((xq_ref, xk_ref, xv_ref)):
        xs_ref[SUBLANES:SUBLANES + n_sub * cs, part * B_WIDTH:(part + 1) * B_WIDTH] = ref[...]

    def activated(row0, part, h):
        cols = slice(part * B_WIDTH + h * hd, part * B_WIDTH + (h + 1) * hd)
        ext = xs_ref[row0:row0 + SUBLANES + cs, cols]
        back = lambda k: (pltpu.roll(ext, k, axis=0) if k else ext)[SUBLANES:]
        y = back(CONV_W - 1) * cw_ref[0:1, cols]
        for t in range(1, CONV_W):
            y = y + back(CONV_W - 1 - t) * cw_ref[t:t + 1, cols]
        return _silu(y)

    def l2norm(t):
        return t * lax.rsqrt(jnp.sum(t * t, axis=-1, keepdims=True) + NORM_EPS)

    ri = lax.broadcasted_iota(jnp.int32, (cs, cs), 0)
    ci = lax.broadcasted_iota(jnp.int32, (cs, cs), 1)
    causal = ri >= ci
    strict = ri > ci
    eye = (ri == ci).astype(F32)

    pre = []
    for sub in range(n_sub):
        row0 = sub * cs
        ab = ab_ref[row0:row0 + cs, :]
        g_all = -jnp.exp(gp_ref[0:1, :]) * _softplus(ab + gp_ref[1:2, :])
        beta_all = _sigmoid(ab)
        if n_valid < cs:
            live = lax.broadcasted_iota(jnp.int32, (cs, LANES), 0) < n_valid
            g_all = jnp.where(live, g_all, 0.0)
            beta_all = jnp.where(live, beta_all, 0.0)
        cum = jnp.dot(causal.astype(F32), g_all, precision=HIGHEST, preferred_element_type=F32)
        cum_sq = jnp.concatenate([cum, jnp.zeros((LANES - cs, LANES), F32)], axis=0) if cs < LANES else cum
        cum_t = cum_sq.T
        ecum = jnp.exp(cum)
        cum_last = cum[cs - 1:cs, :]
        tail = jnp.exp(cum_last - cum)
        elast = jnp.exp(cum_last)
        for h in heads:
            qh = l2norm(activated(row0, 0, h)) * (hd ** -0.5)
            kh = l2norm(activated(row0, 1, h))
            vh = activated(row0, 2, h)
            beta = beta_all[:, SUBLANES + h:SUBLANES + h + 1]
            ec = ecum[:, h:h + 1]
            decay = jnp.exp(jnp.where(causal, cum[:, h:h + 1] - cum_t[h:h + 1, 0:cs], NEG))
            kb = kh.astype(BF16)
            kk = lax.dot_general(kb, kb, nt_dims, preferred_element_type=F32)
            pre.append(dict(
                sub=sub, h=h, kb=kb, decay=decay, qb=qh.astype(BF16), qe=qh * ec,
                kt=(kh * tail[:, h:h + 1]).astype(BF16), elast=elast[:, h:h + 1],
                low=-jnp.where(strict, beta * kk * decay, 0.0),
                rhs=jnp.concatenate([beta * vh, beta * kh * ec], axis=1)))

    tinvs = [eye + e["low"] for e in pre]
    pbs = [e["low"].astype(BF16) for e in pre]
    for _ in range(int(math.log2(cs)) - 1):
        pbs = [jnp.dot(p, p, preferred_element_type=F32).astype(BF16) for p in pbs]
        tinvs = [t + jnp.dot(t.astype(BF16), p, preferred_element_type=F32) for t, p in zip(tinvs, pbs)]
    tbs = [t.astype(BF16) for t in tinvs]
    sol0 = [jnp.dot(t, e["rhs"].astype(BF16), preferred_element_type=F32) for t, e in zip(tbs, pre)]
    resid = [e["rhs"] - s0 + _dot_split(_split(e["low"]), _split(s0)) for e, s0 in zip(pre, sol0)]
    sols = [s0 + jnp.dot(t, r.astype(BF16), preferred_element_type=F32) for s0, t, r in zip(sol0, tbs, resid)]

    ops = []
    for e, sol in zip(pre, sols):
        solb = sol.astype(BF16)
        qk = lax.dot_general(e["qb"], e["kb"], nt_dims, preferred_element_type=F32) * e["decay"]
        ks = lax.dot_general(e["kt"], solb, (((0,), (0,)), ((), ())), preferred_element_type=F32)
        qs = jnp.dot(qk.astype(BF16), solb, preferred_element_type=F32)
        ops.append(dict(h=e["h"], row0=e["sub"] * cs, elast=e["elast"],
                        m=ks[:, hd:2 * hd].astype(BF16), b=ks[:, 0:hd],
                        c=(e["qe"] - qs[:, hd:2 * hd]).astype(BF16), d=qs[:, 0:hd]))

    for o in ops:
        h, row0 = o["h"], o["row0"]
        state = st_ref[h]
        sb = state.astype(BF16)
        o_ref[row0:row0 + cs, h * hd:(h + 1) * hd] = o["d"] + jnp.dot(o["c"], sb, preferred_element_type=F32)
        st_ref[h] = o["elast"] * state + o["b"] - jnp.dot(o["m"], sb, preferred_element_type=F32)

    xs_ref[0:SUBLANES, :] = xs_ref[n_sub * cs:n_sub * cs + SUBLANES, :]


def _delta(p3, col_qkv, col_ab, hist, state0, conv_w, gate_par, cs, n_sub, n_valid):
    nb, t, _ = p3.shape
    rows = cs * n_sub
    qkv_w = 3 * B_WIDTH

    def xspec(part):
        return pl.BlockSpec((None, rows, B_WIDTH), lambda b, c: (b, c, col_qkv // B_WIDTH + part))

    return pl.pallas_call(
        functools.partial(_delta_kernel, cs=cs, n_valid=n_valid),
        grid=(nb, t // rows),
        in_specs=[xspec(0), xspec(1), xspec(2),
                  pl.BlockSpec((None, rows, LANES), lambda b, c: (b, c, col_ab // LANES)),
                  pl.BlockSpec((None, SUBLANES, qkv_w), lambda b, c: (b, 0, 0)),
                  pl.BlockSpec((None, B_HEADS, B_HEAD_DIM, B_HEAD_DIM), lambda b, c: (b, 0, 0, 0)),
                  pl.BlockSpec((CONV_W, qkv_w), lambda b, c: (0, 0)),
                  pl.BlockSpec((SUBLANES, LANES), lambda b, c: (0, 0))],
        out_specs=[pl.BlockSpec((None, rows, B_WIDTH), lambda b, c: (b, c, 0)),
                   pl.BlockSpec((None, B_HEADS, B_HEAD_DIM, B_HEAD_DIM), lambda b, c: (b, 0, 0, 0))],
        out_shape=[jax.ShapeDtypeStruct((nb, t, B_WIDTH), F32),
                   jax.ShapeDtypeStruct((nb, B_HEADS, B_HEAD_DIM, B_HEAD_DIM), F32)],
        scratch_shapes=[pltpu.VMEM((SUBLANES + rows, qkv_w), F32)],
        compiler_params=pltpu.CompilerParams(
            dimension_semantics=("arbitrary", "arbitrary"), vmem_limit_bytes=VMEM_LIMIT),
        name="delta",
    )(p3, p3, p3, p3, hist, state0, conv_w, gate_par)


def _back_kernel(x_ref, gate_ref, oa_ref, za_ref, ob_ref, zb_ref, ga_ref, gb_ref,
                 wa_ref, wb_ref, wo_ref, bn_ref, lg_ref, lb_ref, y_ref):
    ya = (oa_ref[...] * _silu(za_ref[...])).astype(BF16)
    parts = []
    for h in range(B_HEADS):
        cols = slice(h * B_HEAD_DIM, (h + 1) * B_HEAD_DIM)
        t = ob_ref[:, cols]
        t = t * lax.rsqrt(jnp.mean(t * t, axis=-1, keepdims=True) + NORM_EPS) * bn_ref[...]
        parts.append((t * _silu(zb_ref[:, cols])).astype(BF16))
    yb = jnp.concatenate(parts, axis=1)
    ma = jnp.dot(ya, wa_ref[...], preferred_element_type=F32)
    mb = jnp.dot(yb, wb_ref[...], preferred_element_type=F32)
    merged = _sigmoid(ga_ref[...]) * ma + _sigmoid(gb_ref[...]) * mb
    r = ALPHA * x_ref[...] + gate_ref[...] * jnp.dot(merged.astype(BF16), wo_ref[...], preferred_element_type=F32)
    mu = jnp.mean(r, axis=-1, keepdims=True)
    rc = r - mu
    var = jnp.mean(rc * rc, axis=-1, keepdims=True)
    y_ref[...] = rc * lax.rsqrt(var + LN_EPS) * lg_ref[...] + lb_ref[...]


def _back(x, gate, oa, p, ob, wa, wb, wo, bn, lg, lb, tm):
    m, d = x.shape
    groups, r, _ = gate.shape
    blocks_per_group = (m // tm) // groups

    def const(shape):
        return pl.BlockSpec(shape, lambda i: (0,) * len(shape))

    return pl.pallas_call(
        _back_kernel,
        grid=(m // tm,),
        in_specs=[pl.BlockSpec((tm, d), lambda i: (i, 0)),
                  pl.BlockSpec((None, r, d), lambda i: (i // blocks_per_group, 0, 0)),
                  pl.BlockSpec((tm, A_GROUP_WIDTH), lambda i: (i, 0)),
                  pl.BlockSpec((tm, A_GROUP_WIDTH), lambda i: (i, COL_ZA // A_GROUP_WIDTH)),
                  pl.BlockSpec((tm, B_WIDTH), lambda i: (i, 0)),
                  pl.BlockSpec((tm, B_WIDTH), lambda i: (i, COL_ZB // B_WIDTH)),
                  pl.BlockSpec((tm, d), lambda i: (i, COL_GA // d)),
                  pl.BlockSpec((tm, d), lambda i: (i, COL_GB // d)),
                  const(wa.shape), const(wb.shape), const(wo.shape),
                  const((1, B_HEAD_DIM)), const((1, d)), const((1, d))],
        out_specs=pl.BlockSpec((tm, d), lambda i: (i, 0)),
        out_shape=jax.ShapeDtypeStruct((m, d), F32),
        compiler_params=pltpu.CompilerParams(
            dimension_semantics=("arbitrary",), vmem_limit_bytes=VMEM_LIMIT),
        name="back",
    )(x, gate, oa, p, ob, p, p, p, wa, wb, wo, bn, lg, lb)


def _t5_causal_buckets(dist):
    max_exact = N_BUCKETS // 2
    dist = np.asarray(dist, dtype=np.int64)
    ratio = np.maximum(dist, max_exact) / max_exact
    large = max_exact + (np.log(ratio) / math.log(BUCKET_MAX_DIST / max_exact) * (N_BUCKETS - max_exact)).astype(np.int64)
    return np.where(dist < max_exact, dist, np.minimum(large, N_BUCKETS - 1)).astype(np.int32)


def _group_bias(rel_bias, gi):
    window, dil = A_GROUPS[gi]
    buckets = _t5_causal_buckets(dil * np.arange(window // dil + 1))
    onehot = (buckets[:, None] == np.arange(N_BUCKETS)[None, :]).astype(np.float32)
    sel = jnp.dot(onehot, rel_bias[:, gi * A_HEADS:(gi + 1) * A_HEADS], precision=HIGHEST)
    return sel.T


def _neg(*shape):
    return jnp.full(shape, NEG, F32)


def _prompt_tables(rel_bias):
    n = A_KEYS
    tabs = []
    for gi in range(len(A_GROUPS)):
        b = _group_bias(rel_bias, gi)
        c = jnp.concatenate([_neg(A_HEADS, n), b[:, ::-1], _neg(A_HEADS, n - 1)], axis=1)
        flat = jnp.tile(c[:, 1:], (1, n + 1))[:, 0:n * 3 * n]
        hank = flat.reshape(A_HEADS, n, 3 * n)[:, :, 0:2 * n]
        tab = hank[:, ::-1, :]
        tabs.append(jnp.stack([tab, jnp.concatenate([_neg(A_HEADS, n, n), tab[:, :, n:]], axis=2)]))
    return jnp.stack(tabs)


def _sample_tables(rel_bias, n_new):
    tabs_c, tabs_n = [], []
    for gi, (window, dil) in enumerate(A_GROUPS):
        b = _group_bias(rel_bias, gi)
        rev = b[:, :0:-1]
        if dil > 1:
            rev = jnp.concatenate([rev[:, :, None], _neg(A_HEADS, A_KEYS, dil - 1)], axis=2).reshape(A_HEADS, window)
        rows_c, rows_n = [], []
        for i in range(n_new):
            rows_c.append(jnp.concatenate([_neg(A_HEADS, i), rev[:, 0:window - i]], axis=1) if i else rev)
            cols = [b[:, (i - j) // dil][:, None] if (j <= i and (i - j) % dil == 0) else _neg(A_HEADS, 1)
                    for j in range(n_new)]
            rows_n.append(jnp.concatenate(cols + [_neg(A_HEADS, LANES - n_new)], axis=1))
        tabs_c.append(jnp.concatenate(rows_c, axis=0))
        tabs_n.append(jnp.concatenate(rows_n, axis=0))
    return tabs_c, jnp.stack(tabs_n)


def _regroup_w_in(w_in):
    main = w_in[:, 0:9216]
    ab = w_in[:, 9216:9232]
    gates = w_in[:, 9232:11280]
    pad = jnp.zeros((w_in.shape[0], PROJ_COLS - (COL_AB + 2 * B_HEADS)), w_in.dtype)
    return jnp.concatenate([main, gates, ab, pad], axis=1).astype(BF16)


def _cache_from_time_minor(t):
    bsz, _, _, w = t.shape
    return jnp.transpose(t.reshape(bsz, 2, A_HEADS, A_HEAD_DIM, w), (0, 4, 1, 2, 3))[None]


def kernel(x_prompt, x_sample, c_prompt, c_sample, cache_kv_w128, cache_kv_w512, cache_kv_w2048, state_conv, state_delta, w_cond, b_cond, w_in, rel_bias, conv_w, a_log, dt_bias, b_norm_w, w_branch_a, w_branch_b, w_out, ln_g, ln_b):
    bsz, seq, d = x_prompt.shape
    dbsz, dseq, _ = x_sample.shape
    layer = 0

    c_all = jnp.concatenate([c_prompt, c_sample], axis=0)
    c_all = jnp.pad(c_all, ((0, -(bsz + dbsz) % 16), (0, 0)))
    cond = _cond(c_all, w_cond[layer], b_cond[layer])
    shift, scale, gate = cond[:, 0:d], cond[:, d:2 * d], cond[:, 2 * d:3 * d]

    w_in_r = _regroup_w_in(w_in[layer])
    wa = w_branch_a[layer].astype(BF16)
    wb = w_branch_b[layer].astype(BF16)
    wo = w_out[layer].astype(BF16)
    bn = b_norm_w[layer].reshape(1, B_HEAD_DIM)
    lg = ln_g[layer].reshape(1, d)
    lb = ln_b[layer].reshape(1, d)
    gate_par = jnp.zeros((SUBLANES, LANES), F32)
    gate_par = gate_par.at[0, 0:B_HEADS].set(a_log[layer]).at[1, 0:B_HEADS].set(dt_bias[layer])
    cw = conv_w[layer]

    xp = x_prompt.reshape(bsz * seq, d)
    pp = _proj(xp, scale[0:bsz, None, :], shift[0:bsz, None, :], w_in_r, tm=1024)
    pp3 = pp.reshape(bsz, seq, PROJ_COLS)
    oa_p, *tails_p = _attn_prompt(pp3, _prompt_tables(rel_bias))
    hist0 = jnp.zeros((bsz, SUBLANES, QKVB_COLS), F32)
    st0 = jnp.zeros((bsz, B_HEADS, B_HEAD_DIM, B_HEAD_DIM), F32)
    ob_p, st_p = _delta(pp3, COL_QKVB, COL_AB, hist0, st0, cw, gate_par, B_CHUNK, 4, B_CHUNK)
    y_p = _back(xp, gate[0:bsz, None, :], oa_p.reshape(bsz * seq, A_GROUP_WIDTH), pp,
                ob_p.reshape(bsz * seq, B_WIDTH), wa, wb, wo, bn, lg, lb, tm=512)
    y_prompt = y_p.reshape(bsz, seq, d)
    kv_p = [_cache_from_time_minor(t) for t in tails_p]
    conv_p = pp3[:, seq - (CONV_W - 1):, COL_QKVB:COL_QKVB + QKVB_COLS][None]
    delta_p = st_p[None]

    ms = dbsz * dseq
    xs = x_sample.reshape(ms, d)
    rep = lambda t: jnp.repeat(t[bsz:bsz + dbsz], dseq, axis=0)[None]
    ps = _proj(xs, rep(scale), rep(shift), w_in_r, tm=ms)
    ps3 = ps.reshape(dbsz, dseq, PROJ_COLS)
    cache_in = (cache_kv_w128, cache_kv_w512, cache_kv_w2048)
    caches = [jnp.transpose(c[layer], (0, 2, 3, 4, 1)).reshape(dbsz, 2 * A_GROUP_WIDTH, c.shape[2]) for c in cache_in]
    qkv_s = jnp.pad(ps3[:, :, 0:COL_ZA], ((0, 0), (0, 16 - dseq), (0, 0)))
    tabs_c, tab_n = _sample_tables(rel_bias, dseq)
    oa_s, n0, n1, n2 = _attn_sample(qkv_s, dseq, caches[0], caches[1], caches[2], tabs_c, tab_n)
    kv_s = [_cache_from_time_minor(n.reshape(dbsz, 2, A_GROUP_WIDTH, n.shape[2])) for n in (n0, n1, n2)]
    cs_s = SUBLANES
    pad_rows = ((0, 0), (0, cs_s - dseq), (0, 0))
    hist_s = jnp.pad(state_conv[layer], ((0, 0), (SUBLANES - (CONV_W - 1), 0), (0, 0)))
    pd = jnp.concatenate([ps3[:, :, COL_QKVB:COL_QKVB + QKVB_COLS], ps3[:, :, COL_AB:COL_AB + LANES]], axis=-1)
    ob_s, st_s = _delta(jnp.pad(pd, pad_rows), 0, QKVB_COLS, hist_s, state_delta[layer], cw, gate_par,
                        cs_s, 1, dseq)
    y_s = _back(xs, rep(gate), oa_s.reshape(ms, A_GROUP_WIDTH), ps,
                ob_s[:, 0:dseq].reshape(ms, B_WIDTH), wa, wb, wo, bn, lg, lb, tm=ms)
    y_sample = y_s.reshape(dbsz, dseq, d)
    conv_s = ps3[:, dseq - (CONV_W - 1):, COL_QKVB:COL_QKVB + QKVB_COLS][None]
    delta_s = st_s[None]

    return (y_prompt, y_sample, kv_p[0], kv_p[1], kv_p[2], conv_p, delta_p,
            kv_s[0], kv_s[1], kv_s[2], conv_s, delta_s)
```

```python
import functools
import math

import numpy as np
import jax
import jax.numpy as jnp
from jax import lax
from jax.experimental import pallas as pl
from jax.experimental.pallas import tpu as pltpu

F32 = jnp.float32
BF16 = jnp.bfloat16
HIGHEST = lax.Precision.HIGHEST

D_MODEL = 1024
A_GROUPS = ((128, 1), (512, 4), (2048, 16))
A_HEADS = 8
A_HEAD_DIM = 64
A_GROUP_WIDTH = A_HEADS * A_HEAD_DIM
A_KEYS = 128
N_BUCKETS = 32
BUCKET_MAX_DIST = 2048
B_HEADS = 8
B_HEAD_DIM = 128
B_WIDTH = B_HEADS * B_HEAD_DIM
CONV_W = 4
B_CHUNK = 64
DEPTH = 1
ALPHA = (2 * DEPTH) ** 0.25
LN_EPS = 1e-5
NORM_EPS = 1e-6
NEG = -1e30

LANES = 128
SUBLANES = 8

COL_QA = 0
COL_KA = 1536
COL_VA = 3072
COL_ZA = 4608
COL_QKVB = 5120
COL_ZB = 8192
COL_GA = 9216
COL_GB = 10240
COL_AB = 11264
PROJ_COLS = 11520
QKVB_COLS = 3 * B_WIDTH
PROJ_TN = 2304

VMEM_LIMIT = 48 * 1024 * 1024


def _sigmoid(x):
    return 1.0 / (1.0 + jnp.exp(-x))


def _silu(x):
    return x * _sigmoid(x)


def _softplus(x):
    return jnp.maximum(x, 0.0) + jnp.log1p(jnp.exp(-jnp.abs(x)))


def _split(x):
    hi = x.astype(BF16)
    return hi, (x - hi.astype(F32)).astype(BF16)


def _dot_split(a, b):
    lhs = jnp.concatenate([a[0], a[1], a[0]], axis=1)
    rhs = jnp.concatenate([b[0], b[0], b[1]], axis=0)
    return jnp.dot(lhs, rhs, preferred_element_type=F32)


def _cond_kernel(c_ref, w_ref, b_ref, o_ref):
    s = _silu(c_ref[...]).astype(BF16)
    o_ref[...] = jnp.dot(s, w_ref[...].astype(BF16), preferred_element_type=F32) + b_ref[...]


def _cond(c, w, b):
    n, d = c.shape
    width = w.shape[1]
    tn = 512
    return pl.pallas_call(
        _cond_kernel,
        grid=(width // tn,),
        in_specs=[pl.BlockSpec((n, d), lambda j: (0, 0)),
                  pl.BlockSpec((d, tn), lambda j: (0, j)),
                  pl.BlockSpec((1, tn), lambda j: (0, j))],
        out_specs=pl.BlockSpec((n, tn), lambda j: (0, j)),
        out_shape=jax.ShapeDtypeStruct((n, width), F32),
        name="cond",
    )(c, w, b.reshape(1, width))


def _proj_kernel(x_ref, sc_ref, sh_ref, w_ref, wl_ref, o_ref, h_ref):
    j = pl.program_id(1)
    last = pl.num_programs(1) - 1

    @pl.when(j == 0)
    def _():
        h_ref[...] = (x_ref[...] * (1.0 + sc_ref[...]) + sh_ref[...]).astype(BF16)

    @pl.when(j < last)
    def _():
        o_ref[...] = jnp.dot(h_ref[...], w_ref[...], preferred_element_type=F32)

    @pl.when(j == last)
    def _():
        o_ref[...] = jnp.dot(h_ref[...], wl_ref[...], preferred_element_type=F32)


def _proj(x, scale, shift, w, w_last, tm):
    m, d = x.shape
    n_blocks = PROJ_COLS // PROJ_TN
    groups, r, _ = scale.shape
    blocks_per_group = (m // tm) // groups
    mod_spec = pl.BlockSpec((None, r, d), lambda i, j: (i // blocks_per_group, 0, 0))
    return pl.pallas_call(
        _proj_kernel,
        grid=(m // tm, n_blocks),
        in_specs=[pl.BlockSpec((tm, d), lambda i, j: (i, 0)),
                  mod_spec, mod_spec,
                  pl.BlockSpec((d, PROJ_TN), lambda i, j: (0, jnp.minimum(j, n_blocks - 2))),
                  pl.BlockSpec((d, PROJ_TN), lambda i, j: (0, 0), pipeline_mode=pl.Buffered(1))],
        out_specs=pl.BlockSpec((tm, PROJ_TN), lambda i, j: (i, j)),
        out_shape=jax.ShapeDtypeStruct((m, PROJ_COLS), F32),
        scratch_shapes=[pltpu.VMEM((tm, d), BF16)],
        compiler_params=pltpu.CompilerParams(
            dimension_semantics=("arbitrary", "arbitrary"), vmem_limit_bytes=VMEM_LIMIT),
        name="proj",
    )(x, scale, shift, w, w_last)


ATT_TB = 2048


def _attn_prompt_kernel(q_ref, k_ref, kp_ref, v_ref, vp_ref, tb_ref, o_ref, t0_ref, t1_ref, t2_ref,
                        acc_ref, m_ref, l_ref):
    i = pl.program_id(1)
    gi = pl.program_id(3)
    tb = ATT_TB
    tail_refs = (t0_ref, t1_ref, t2_ref)

    lane = lax.broadcasted_iota(jnp.int32, (A_KEYS, LANES), 1)
    lo = lane < A_HEAD_DIM

    def run_group(d, first, last):
        tq = A_KEYS * d

        def body(sb, r):
            start = sb * tq + r
            rows = pl.ds(start, A_KEYS, stride=d)
            q = q_ref[rows, :]
            if sb == 0:
                prows = pl.ds(tb - tq + r, A_KEYS, stride=d)
                kprev, vprev = kp_ref[prows, :], vp_ref[prows, :]
            else:
                prows = pl.ds(start - tq, A_KEYS, stride=d)
                kprev, vprev = k_ref[prows, :], v_ref[prows, :]
            kk = jnp.concatenate([kprev, k_ref[rows, :]], axis=0).astype(BF16)
            vv = jnp.concatenate([vprev, v_ref[rows, :]], axis=0).astype(BF16)
            variant = (i == 0).astype(jnp.int32) if sb == 0 else 0
            res = []
            for hh in range(2):
                sel = lo if hh == 0 else jnp.logical_not(lo)
                qh = jnp.where(sel, q * (A_HEAD_DIM ** -0.5), 0.0).astype(BF16)
                s = lax.dot_general(qh, kk, (((1,), (1,)), ((), ())), preferred_element_type=F32)
                s = s + tb_ref[variant, hh]
                mx = jnp.max(s, axis=-1, keepdims=True)
                p = jnp.exp(s - mx)
                ls = jnp.sum(p, axis=-1, keepdims=True)
                o = jnp.dot(p.astype(BF16), vv, preferred_element_type=F32)
                res.append((o, mx, ls))
            o = jnp.where(lo, res[0][0], res[1][0])
            mx = jnp.where(lo, res[0][1], res[1][1])
            ls = jnp.where(lo, res[0][2], res[1][2])
            if first:
                acc_ref[rows, :] = o
                m_ref[rows, :] = mx
                l_ref[rows, :] = ls
            else:
                mo = m_ref[rows, :]
                mn = jnp.maximum(mo, mx)
                a = jnp.exp(mo - mn)
                b = jnp.exp(mx - mn)
                acc = a * acc_ref[rows, :] + b * o
                ll = a * l_ref[rows, :] + b * ls
                if last:
                    o_ref[rows, :] = acc / ll
                else:
                    acc_ref[rows, :] = acc
                    m_ref[rows, :] = mn
                    l_ref[rows, :] = ll

        for u in range(tb // A_KEYS):
            body(u // d, u % d)

    def emit_tail(t_ref):
        window = t_ref.shape[2]
        for kv, src in enumerate((k_ref, v_ref)):
            for c0 in range(0, window, LANES):
                t_ref[kv, :, c0:c0 + LANES] = src[tb - window + c0:tb - window + c0 + LANES, :].T

    n_groups = len(A_GROUPS)
    for step in range(n_groups):
        g = n_groups - 1 - step

        @pl.when(gi == step)
        def _(g=g, step=step):
            run_group(A_GROUPS[g][1], step == 0, step == n_groups - 1)

        @pl.when(jnp.logical_and(gi == step, i == pl.num_programs(1) - 1))
        def _(g=g):
            emit_tail(tail_refs[g])


def _attn_prompt(p3, tables):
    bsz, seq, _ = p3.shape
    nblk = seq // ATT_TB
    n_groups = len(A_GROUPS)
    hp_per_group = A_GROUP_WIDTH // LANES
    windows = [min(w, ATT_TB) for w, _ in A_GROUPS]
    assert all(min(w, seq) == wt for (w, _), wt in zip(A_GROUPS, windows))

    def tail_idx(b, i, hp, gi):
        return (b, 0, jnp.where(i == nblk - 1, hp, 0), 0)

    def col(base):
        return lambda b, i, hp, gi: (b, i, base // LANES + (n_groups - 1 - gi) * hp_per_group + hp)

    def col_prev(base):
        return lambda b, i, hp, gi: (b, jnp.maximum(i - 1, 0),
                                     base // LANES + (n_groups - 1 - gi) * hp_per_group + hp)

    blk = (None, ATT_TB, LANES)
    return pl.pallas_call(
        _attn_prompt_kernel,
        grid=(bsz, nblk, hp_per_group, n_groups),
        in_specs=[pl.BlockSpec(blk, col(COL_QA)),
                  pl.BlockSpec(blk, col(COL_KA)),
                  pl.BlockSpec(blk, col_prev(COL_KA)),
                  pl.BlockSpec(blk, col(COL_VA)),
                  pl.BlockSpec(blk, col_prev(COL_VA)),
                  pl.BlockSpec((None, 2, 2, A_KEYS, 2 * A_KEYS),
                               lambda b, i, hp, gi: (n_groups - 1 - gi, 0, hp, 0, 0))],
        out_specs=[pl.BlockSpec(blk, lambda b, i, hp, gi: (b, i, hp))]
        + [pl.BlockSpec((None, 2, LANES, w), tail_idx) for w in windows],
        out_shape=[jax.ShapeDtypeStruct((bsz, seq, A_GROUP_WIDTH), F32)]
        + [jax.ShapeDtypeStruct((bsz, 2, A_GROUP_WIDTH, w), F32) for w in windows],
        scratch_shapes=[pltpu.VMEM((ATT_TB, LANES), F32),
                        pltpu.VMEM((ATT_TB, LANES), F32),
                        pltpu.VMEM((ATT_TB, LANES), F32)],
        compiler_params=pltpu.CompilerParams(
            dimension_semantics=("arbitrary",) * 4, vmem_limit_bytes=VMEM_LIMIT),
        name="attn_prompt",
    )(p3, p3, p3, p3, p3, tables)


ROLL_ELEMS = 64 * SUBLANES * LANES


def _attn_sample_kernel(qkv_ref, c0_ref, c1_ref, c2_ref, tc0_ref, tc1_ref, tc2_ref, tn_ref,
                        o_ref, n0_ref, n1_ref, n2_ref, nt_ref, *, n_new):
    caches = (c0_ref, c1_ref, c2_ref)
    tables = (tc0_ref, tc1_ref, tc2_ref)
    outs = (n0_ref, n1_ref, n2_ref)
    gw = A_GROUP_WIDTH
    scale = A_HEAD_DIM ** -0.5

    head_of_lane = lax.broadcasted_iota(jnp.int32, (A_HEADS, gw), 1) >> int(math.log2(A_HEAD_DIM))
    head_of_row = lax.broadcasted_iota(jnp.int32, (A_HEADS, gw), 0)
    hmask = head_of_lane == head_of_row

    parts = []
    for g, (window, d) in enumerate(A_GROUPS):
        c_ref, n_ref = caches[g], outs[g]
        q_new = qkv_ref[:, COL_QA + g * gw:COL_QA + (g + 1) * gw]
        k_new = qkv_ref[:, COL_KA + g * gw:COL_KA + (g + 1) * gw]
        v_new = qkv_ref[:, COL_VA + g * gw:COL_VA + (g + 1) * gw]

        qbd = jnp.concatenate([jnp.where(hmask, q_new[i:i + 1, :], 0.0) for i in range(n_new)], axis=0)
        qb = qbd.astype(BF16)
        s_c = jnp.dot(qb, c_ref[0:gw, :].astype(BF16), preferred_element_type=F32) * scale + tables[g][...]
        qf = qb.astype(F32)
        kf = k_new.astype(BF16).astype(F32)
        vf = v_new.astype(BF16).astype(F32)
        s_n = [jnp.sum(qf * kf[j:j + 1, :], axis=-1, keepdims=True) * scale + tn_ref[g, :, j:j + 1]
               for j in range(n_new)]
        mx = functools.reduce(jnp.maximum, s_n, jnp.max(s_c, axis=-1, keepdims=True))
        p_c = jnp.exp(s_c - mx)
        p_n = [jnp.exp(s - mx) for s in s_n]
        ls = jnp.sum(p_c, axis=-1, keepdims=True) + sum(p_n)
        o = lax.dot_general(p_c.astype(BF16), c_ref[gw:2 * gw, :].astype(BF16), (((1,), (1,)), ((), ())),
                            preferred_element_type=F32)
        for j in range(n_new):
            o = o + p_n[j].astype(BF16).astype(F32) * vf[j:j + 1, :]
        parts.append((o, mx, ls))

        for cb in range(2 * gw // LANES):
            src = k_new if cb < gw // LANES else v_new
            c0 = (cb % (gw // LANES)) * LANES
            blk = jnp.concatenate([src[:, c0:c0 + LANES], jnp.zeros((LANES - src.shape[0], LANES), F32)], axis=0)
            nt_ref[cb * LANES:(cb + 1) * LANES, :] = pltpu.roll(blk.T, LANES - n_new, axis=1)

        nrows = min(2 * gw, ROLL_ELEMS // window)
        is_new = lax.broadcasted_iota(jnp.int32, (nrows, LANES), 1) >= LANES - n_new
        for r0 in range(0, 2 * gw, nrows):
            rows = slice(r0, r0 + nrows)
            y = pltpu.roll(c_ref[rows, :], window - n_new, axis=1)
            if window > LANES:
                n_ref[rows, 0:window - LANES] = y[:, 0:window - LANES]
            n_ref[rows, window - LANES:window] = jnp.where(is_new, nt_ref[rows, :], y[:, window - LANES:window])

    mt = functools.reduce(jnp.maximum, [p[1] for p in parts])
    num = sum(jnp.exp(p[1] - mt) * p[0] for p in parts)
    den = sum(jnp.exp(p[1] - mt) * p[2] for p in parts)
    comb = num / den
    for i in range(n_new):
        sel = jnp.where(hmask, comb[i * A_HEADS:(i + 1) * A_HEADS, :], 0.0)
        o_ref[i:i + 1, :] = jnp.sum(sel, axis=0, keepdims=True)


def _attn_sample(qkv, n_new, c0, c1, c2, tabs_c, tab_n):
    dbsz, rows, qkv_w = qkv.shape
    nq = n_new * A_HEADS

    def cspec(w):
        return pl.BlockSpec((None, 2 * A_GROUP_WIDTH, w), lambda b: (b, 0, 0))

    ws = [g[0] for g in A_GROUPS]
    return pl.pallas_call(
        functools.partial(_attn_sample_kernel, n_new=n_new),
        grid=(dbsz,),
        in_specs=[pl.BlockSpec((None, rows, qkv_w), lambda b: (b, 0, 0)),
                  cspec(ws[0]), cspec(ws[1]), cspec(ws[2])]
        + [pl.BlockSpec((nq, w), lambda b: (0, 0)) for w in ws]
        + [pl.BlockSpec((len(A_GROUPS), nq, LANES), lambda b: (0, 0, 0))],
        out_specs=[pl.BlockSpec((None, n_new, A_GROUP_WIDTH), lambda b: (b, 0, 0)),
                   cspec(ws[0]), cspec(ws[1]), cspec(ws[2])],
        out_shape=[jax.ShapeDtypeStruct((dbsz, n_new, A_GROUP_WIDTH), F32)]
        + [jax.ShapeDtypeStruct(c.shape, c.dtype) for c in (c0, c1, c2)],
        scratch_shapes=[pltpu.VMEM((2 * A_GROUP_WIDTH, LANES), F32)],
        compiler_params=pltpu.CompilerParams(
            dimension_semantics=("arbitrary",), vmem_limit_bytes=VMEM_LIMIT),
        name="attn_sample",
    )(qkv, c0, c1, c2, *tabs_c, tab_n)


def _delta_kernel(xq_ref, xk_ref, xv_ref, ab_ref, hist_ref, s0_ref, cw_ref, gp_ref, o_ref, st_ref, xs_ref,
                  *, cs, n_valid):
    c = pl.program_id(1)
    n_sub = xq_ref.shape[0] // cs
    hd = B_HEAD_DIM
    heads = range(B_HEADS)
    nt_dims = (((1,), (1,)), ((), ()))

    @pl.when(c == 0)
    def _():
        st_ref[...] = s0_ref[...]
        xs_ref[0:SUBLANES, :] = hist_ref[...]

    for part, ref in enumerate((xq_ref, xk_ref, xv_ref)):
        xs_ref[SUBLANES:SUBLANES + n_sub * cs, part * B_WIDTH:(part + 1) * B_WIDTH] = ref[...]

    def activated(row0, part, h):
        cols = slice(part * B_WIDTH + h * hd, part * B_WIDTH + (h + 1) * hd)
        ext = xs_ref[row0:row0 + SUBLANES + cs, cols]
        back = lambda k: (pltpu.roll(ext, k, axis=0) if k else ext)[SUBLANES:]
        y = back(CONV_W - 1) * cw_ref[0:1, cols]
        for t in range(1, CONV_W):
            y = y + back(CONV_W - 1 - t) * cw_ref[t:t + 1, cols]
        return _silu(y)

    def l2norm(t):
        return t * lax.rsqrt(jnp.sum(t * t, axis=-1, keepdims=True) + NORM_EPS)

    ri = lax.broadcasted_iota(jnp.int32, (cs, cs), 0)
    ci = lax.broadcasted_iota(jnp.int32, (cs, cs), 1)
    causal = ri >= ci
    strict = ri > ci
    eye = (ri == ci).astype(F32)

    pre = []
    for sub in range(n_sub):
        row0 = sub * cs
        ab = ab_ref[row0:row0 + cs, :]
        g_all = -jnp.exp(gp_ref[0:1, :]) * _softplus(ab + gp_ref[1:2, :])
        beta_all = _sigmoid(ab)
        if n_valid < cs:
            live = lax.broadcasted_iota(jnp.int32, (cs, LANES), 0) < n_valid
            g_all = jnp.where(live, g_all, 0.0)
            beta_all = jnp.where(live, beta_all, 0.0)
        cum = jnp.dot(causal.astype(F32), g_all, precision=HIGHEST, preferred_element_type=F32)
        cum_sq = jnp.concatenate([cum, jnp.zeros((LANES - cs, LANES), F32)], axis=0) if cs < LANES else cum
        cum_t = cum_sq.T
        ecum = jnp.exp(cum)
        cum_last = cum[cs - 1:cs, :]
        tail = jnp.exp(cum_last - cum)
        elast = jnp.exp(cum_last)
        for h in heads:
            qh = l2norm(activated(row0, 0, h)) * (hd ** -0.5)
            kh = l2norm(activated(row0, 1, h))
            vh = activated(row0, 2, h)
            beta = beta_all[:, SUBLANES + h:SUBLANES + h + 1]
            ec = ecum[:, h:h + 1]
            decay = jnp.exp(jnp.where(causal, cum[:, h:h + 1] - cum_t[h:h + 1, 0:cs], NEG))
            kb = kh.astype(BF16)
            kk = lax.dot_general(kb, kb, nt_dims, preferred_element_type=F32)
            pre.append(dict(
                sub=sub, h=h, kb=kb, decay=decay, qb=qh.astype(BF16), qe=qh * ec,
                kt=(kh * tail[:, h:h + 1]).astype(BF16), elast=elast[:, h:h + 1],
                low=-jnp.where(strict, beta * kk * decay, 0.0),
                rhs=jnp.concatenate([beta * vh, beta * kh * ec], axis=1)))

    tinvs = [eye + e["low"] for e in pre]
    pbs = [e["low"].astype(BF16) for e in pre]
    for _ in range(int(math.log2(cs)) - 1):
        pbs = [jnp.dot(p, p, preferred_element_type=F32).astype(BF16) for p in pbs]
        tinvs = [t + jnp.dot(t.astype(BF16), p, preferred_element_type=F32) for t, p in zip(tinvs, pbs)]
    tbs = [t.astype(BF16) for t in tinvs]
    sol0 = [jnp.dot(t, e["rhs"].astype(BF16), preferred_element_type=F32) for t, e in zip(tbs, pre)]
    resid = [e["rhs"] - s0 + _dot_split(_split(e["low"]), _split(s0)) for e, s0 in zip(pre, sol0)]
    sols = [s0 + jnp.dot(t, r.astype(BF16), preferred_element_type=F32) for s0, t, r in zip(sol0, tbs, resid)]

    ops = []
    for e, sol in zip(pre, sols):
        solb = sol.astype(BF16)
        qk = lax.dot_general(e["qb"], e["kb"], nt_dims, preferred_element_type=F32) * e["decay"]
        ks = lax.dot_general(e["kt"], solb, (((0,), (0,)), ((), ())), preferred_element_type=F32)
        qs = jnp.dot(qk.astype(BF16), solb, preferred_element_type=F32)
        ops.append(dict(h=e["h"], row0=e["sub"] * cs, elast=e["elast"],
                        m=ks[:, hd:2 * hd].astype(BF16), b=ks[:, 0:hd],
                        c=(e["qe"] - qs[:, hd:2 * hd]).astype(BF16), d=qs[:, 0:hd]))

    for o in ops:
        h, row0 = o["h"], o["row0"]
        state = st_ref[h]
        sb = state.astype(BF16)
        o_ref[row0:row0 + cs, h * hd:(h + 1) * hd] = o["d"] + jnp.dot(o["c"], sb, preferred_element_type=F32)
        st_ref[h] = o["elast"] * state + o["b"] - jnp.dot(o["m"], sb, preferred_element_type=F32)

    xs_ref[0:SUBLANES, :] = xs_ref[n_sub * cs:n_sub * cs + SUBLANES, :]


def _delta(p3, col_qkv, col_ab, hist, state0, conv_w, gate_par, cs, n_sub, n_valid):
    nb, t, _ = p3.shape
    rows = cs * n_sub
    qkv_w = 3 * B_WIDTH

    def xspec(part):
        return pl.BlockSpec((None, rows, B_WIDTH), lambda b, c: (b, c, col_qkv // B_WIDTH + part))

    return pl.pallas_call(
        functools.partial(_delta_kernel, cs=cs, n_valid=n_valid),
        grid=(nb, t // rows),
        in_specs=[xspec(0), xspec(1), xspec(2),
                  pl.BlockSpec((None, rows, LANES), lambda b, c: (b, c, col_ab // LANES)),
                  pl.BlockSpec((None, SUBLANES, qkv_w), lambda b, c: (b, 0, 0)),
                  pl.BlockSpec((None, B_HEADS, B_HEAD_DIM, B_HEAD_DIM), lambda b, c: (b, 0, 0, 0)),
                  pl.BlockSpec((CONV_W, qkv_w), lambda b, c: (0, 0)),
                  pl.BlockSpec((SUBLANES, LANES), lambda b, c: (0, 0))],
        out_specs=[pl.BlockSpec((None, rows, B_WIDTH), lambda b, c: (b, c, 0)),
                   pl.BlockSpec((None, B_HEADS, B_HEAD_DIM, B_HEAD_DIM), lambda b, c: (b, 0, 0, 0))],
        out_shape=[jax.ShapeDtypeStruct((nb, t, B_WIDTH), F32),
                   jax.ShapeDtypeStruct((nb, B_HEADS, B_HEAD_DIM, B_HEAD_DIM), F32)],
        scratch_shapes=[pltpu.VMEM((SUBLANES + rows, qkv_w), F32)],
        compiler_params=pltpu.CompilerParams(
            dimension_semantics=("arbitrary", "arbitrary"), vmem_limit_bytes=VMEM_LIMIT),
        name="delta",
    )(p3, p3, p3, p3, hist, state0, conv_w, gate_par)


def _back_kernel(x_ref, gate_ref, oa_ref, za_ref, ob_ref, zb_ref, ga_ref, gb_ref,
                 wa_ref, wb_ref, wo_ref, bn_ref, lg_ref, lb_ref, y_ref):
    ya = (oa_ref[...] * _silu(za_ref[...])).astype(BF16)
    parts = []
    for h in range(B_HEADS):
        cols = slice(h * B_HEAD_DIM, (h + 1) * B_HEAD_DIM)
        t = ob_ref[:, cols]
        t = t * lax.rsqrt(jnp.mean(t * t, axis=-1, keepdims=True) + NORM_EPS) * bn_ref[...]
        parts.append((t * _silu(zb_ref[:, cols])).astype(BF16))
    yb = jnp.concatenate(parts, axis=1)
    ma = jnp.dot(ya, wa_ref[...], preferred_element_type=F32)
    mb = jnp.dot(yb, wb_ref[...], preferred_element_type=F32)
    merged = _sigmoid(ga_ref[...]) * ma + _sigmoid(gb_ref[...]) * mb
    r = ALPHA * x_ref[...] + gate_ref[...] * jnp.dot(merged.astype(BF16), wo_ref[...], preferred_element_type=F32)
    mu = jnp.mean(r, axis=-1, keepdims=True)
    rc = r - mu
    var = jnp.mean(rc * rc, axis=-1, keepdims=True)
    y_ref[...] = rc * lax.rsqrt(var + LN_EPS) * lg_ref[...] + lb_ref[...]


def _back(x, gate, oa, p, ob, wa, wb, wo, bn, lg, lb, tm):
    m, d = x.shape
    groups, r, _ = gate.shape
    blocks_per_group = (m // tm) // groups

    def const(shape):
        return pl.BlockSpec(shape, lambda i: (0,) * len(shape))

    return pl.pallas_call(
        _back_kernel,
        grid=(m // tm,),
        in_specs=[pl.BlockSpec((tm, d), lambda i: (i, 0)),
                  pl.BlockSpec((None, r, d), lambda i: (i // blocks_per_group, 0, 0)),
                  pl.BlockSpec((tm, A_GROUP_WIDTH), lambda i: (i, 0)),
                  pl.BlockSpec((tm, A_GROUP_WIDTH), lambda i: (i, COL_ZA // A_GROUP_WIDTH)),
                  pl.BlockSpec((tm, B_WIDTH), lambda i: (i, 0)),
                  pl.BlockSpec((tm, B_WIDTH), lambda i: (i, COL_ZB // B_WIDTH)),
                  pl.BlockSpec((tm, d), lambda i: (i, COL_GA // d)),
                  pl.BlockSpec((tm, d), lambda i: (i, COL_GB // d)),
                  const(wa.shape), const(wb.shape), const(wo.shape),
                  const((1, B_HEAD_DIM)), const((1, d)), const((1, d))],
        out_specs=pl.BlockSpec((tm, d), lambda i: (i, 0)),
        out_shape=jax.ShapeDtypeStruct((m, d), F32),
        compiler_params=pltpu.CompilerParams(
            dimension_semantics=("arbitrary",), vmem_limit_bytes=VMEM_LIMIT),
        name="back",
    )(x, gate, oa, p, ob, p, p, p, wa, wb, wo, bn, lg, lb)


def _t5_causal_buckets(dist):
    max_exact = N_BUCKETS // 2
    dist = np.asarray(dist, dtype=np.int64)
    ratio = np.maximum(dist, max_exact) / max_exact
    large = max_exact + (np.log(ratio) / math.log(BUCKET_MAX_DIST / max_exact) * (N_BUCKETS - max_exact)).astype(np.int64)
    return np.where(dist < max_exact, dist, np.minimum(large, N_BUCKETS - 1)).astype(np.int32)


def _group_bias(rel_bias, gi):
    window, dil = A_GROUPS[gi]
    buckets = _t5_causal_buckets(dil * np.arange(window // dil + 1))
    onehot = (buckets[:, None] == np.arange(N_BUCKETS)[None, :]).astype(np.float32)
    sel = jnp.dot(onehot, rel_bias[:, gi * A_HEADS:(gi + 1) * A_HEADS], precision=HIGHEST)
    return sel.T


def _neg(*shape):
    return jnp.full(shape, NEG, F32)


def _prompt_tables(rel_bias):
    n = A_KEYS
    tabs = []
    for gi in range(len(A_GROUPS)):
        b = _group_bias(rel_bias, gi)
        c = jnp.concatenate([_neg(A_HEADS, n), b[:, ::-1], _neg(A_HEADS, n - 1)], axis=1)
        flat = jnp.tile(c[:, 1:], (1, n + 1))[:, 0:n * 3 * n]
        hank = flat.reshape(A_HEADS, n, 3 * n)[:, :, 0:2 * n]
        tab = hank[:, ::-1, :]
        tabs.append(jnp.stack([tab, jnp.concatenate([_neg(A_HEADS, n, n), tab[:, :, n:]], axis=2)]))
    return jnp.stack(tabs)


def _sample_tables(rel_bias, n_new):
    tabs_c, tabs_n = [], []
    for gi, (window, dil) in enumerate(A_GROUPS):
        b = _group_bias(rel_bias, gi)
        rev = b[:, :0:-1]
        if dil > 1:
            rev = jnp.concatenate([rev[:, :, None], _neg(A_HEADS, A_KEYS, dil - 1)], axis=2).reshape(A_HEADS, window)
        rows_c, rows_n = [], []
        for i in range(n_new):
            rows_c.append(jnp.concatenate([_neg(A_HEADS, i), rev[:, 0:window - i]], axis=1) if i else rev)
            cols = [b[:, (i - j) // dil][:, None] if (j <= i and (i - j) % dil == 0) else _neg(A_HEADS, 1)
                    for j in range(n_new)]
            rows_n.append(jnp.concatenate(cols + [_neg(A_HEADS, LANES - n_new)], axis=1))
        tabs_c.append(jnp.concatenate(rows_c, axis=0))
        tabs_n.append(jnp.concatenate(rows_n, axis=0))
    return tabs_c, jnp.stack(tabs_n)


def _regroup_w_in(w_in):
    assert COL_GA == PROJ_COLS - PROJ_TN
    wb = w_in.astype(BF16)
    pad = jnp.zeros((w_in.shape[0], PROJ_COLS - (COL_AB + 2 * B_HEADS)), BF16)
    return wb, jnp.concatenate([wb[:, 9232:11280], wb[:, 9216:9232], pad], axis=1)


def _cache_from_time_minor(t):
    bsz, _, _, w = t.shape
    return jnp.transpose(t.reshape(bsz, 2, A_HEADS, A_HEAD_DIM, w), (0, 4, 1, 2, 3))[None]


def kernel(x_prompt, x_sample, c_prompt, c_sample, cache_kv_w128, cache_kv_w512, cache_kv_w2048, state_conv, state_delta, w_cond, b_cond, w_in, rel_bias, conv_w, a_log, dt_bias, b_norm_w, w_branch_a, w_branch_b, w_out, ln_g, ln_b):
    bsz, seq, d = x_prompt.shape
    dbsz, dseq, _ = x_sample.shape
    layer = 0

    c_all = jnp.concatenate([c_prompt, c_sample], axis=0)
    c_all = jnp.pad(c_all, ((0, -(bsz + dbsz) % 16), (0, 0)))
    cond = _cond(c_all, w_cond[layer], b_cond[layer])
    shift, scale, gate = cond[:, 0:d], cond[:, d:2 * d], cond[:, 2 * d:3 * d]

    w_in_b, w_in_last = _regroup_w_in(w_in[layer])
    wa = w_branch_a[layer].astype(BF16)
    wb = w_branch_b[layer].astype(BF16)
    wo = w_out[layer].astype(BF16)
    bn = b_norm_w[layer].reshape(1, B_HEAD_DIM)
    lg = ln_g[layer].reshape(1, d)
    lb = ln_b[layer].reshape(1, d)
    gate_par = jnp.zeros((SUBLANES, LANES), F32)
    gate_par = gate_par.at[0, 0:B_HEADS].set(a_log[layer]).at[1, 0:B_HEADS].set(dt_bias[layer])
    cw = conv_w[layer]

    xp = x_prompt.reshape(bsz * seq, d)
    pp = _proj(xp, scale[0:bsz, None, :], shift[0:bsz, None, :], w_in_b, w_in_last, tm=1024)
    pp3 = pp.reshape(bsz, seq, PROJ_COLS)
    oa_p, *tails_p = _attn_prompt(pp3, _prompt_tables(rel_bias))
    hist0 = jnp.zeros((bsz, SUBLANES, QKVB_COLS), F32)
    st0 = jnp.zeros((bsz, B_HEADS, B_HEAD_DIM, B_HEAD_DIM), F32)
    ob_p, st_p = _delta(pp3, COL_QKVB, COL_AB, hist0, st0, cw, gate_par, B_CHUNK, 4, B_CHUNK)
    y_p = _back(xp, gate[0:bsz, None, :], oa_p.reshape(bsz * seq, A_GROUP_WIDTH), pp,
                ob_p.reshape(bsz * seq, B_WIDTH), wa, wb, wo, bn, lg, lb, tm=512)
    y_prompt = y_p.reshape(bsz, seq, d)
    kv_p = [_cache_from_time_minor(t) for t in tails_p]
    conv_p = pp3[:, seq - (CONV_W - 1):, COL_QKVB:COL_QKVB + QKVB_COLS][None]
    delta_p = st_p[None]

    ms = dbsz * dseq
    xs = x_sample.reshape(ms, d)
    rep = lambda t: jnp.repeat(t[bsz:bsz + dbsz], dseq, axis=0)[None]
    ps = _proj(xs, rep(scale), rep(shift), w_in_b, w_in_last, tm=ms)
    ps3 = ps.reshape(dbsz, dseq, PROJ_COLS)
    cache_in = (cache_kv_w128, cache_kv_w512, cache_kv_w2048)
    caches = [jnp.transpose(c[layer], (0, 2, 3, 4, 1)).reshape(dbsz, 2 * A_GROUP_WIDTH, c.shape[2]) for c in cache_in]
    qkv_s = jnp.pad(ps3[:, :, 0:COL_ZA], ((0, 0), (0, 16 - dseq), (0, 0)))
    tabs_c, tab_n = _sample_tables(rel_bias, dseq)
    oa_s, n0, n1, n2 = _attn_sample(qkv_s, dseq, caches[0], caches[1], caches[2], tabs_c, tab_n)
    kv_s = [_cache_from_time_minor(n.reshape(dbsz, 2, A_GROUP_WIDTH, n.shape[2])) for n in (n0, n1, n2)]
    cs_s = SUBLANES
    pad_rows = ((0, 0), (0, cs_s - dseq), (0, 0))
    hist_s = jnp.pad(state_conv[layer], ((0, 0), (SUBLANES - (CONV_W - 1), 0), (0, 0)))
    pd = jnp.concatenate([ps3[:, :, COL_QKVB:COL_QKVB + QKVB_COLS], ps3[:, :, COL_AB:COL_AB + LANES]], axis=-1)
    ob_s, st_s = _delta(jnp.pad(pd, pad_rows), 0, QKVB_COLS, hist_s, state_delta[layer], cw, gate_par,
                        cs_s, 1, dseq)
    y_s = _back(xs, rep(gate), oa_s.reshape(ms, A_GROUP_WIDTH), ps,
                ob_s[:, 0:dseq].reshape(ms, B_WIDTH), wa, wb, wo, bn, lg, lb, tm=ms)
    y_sample = y_s.reshape(dbsz, dseq, d)
    conv_s = ps3[:, dseq - (CONV_W - 1):, COL_QKVB:COL_QKVB + QKVB_COLS][None]
    delta_s = st_s[None]

    return (y_prompt, y_sample, kv_p[0], kv_p[1], kv_p[2], conv_p, delta_p,
            kv_s[0], kv_s[1], kv_s[2], conv_s, delta_s)
```
